```python
import math, functools
import jax, jax.numpy as jnp
from jax import lax
import numpy as np

D_MODEL = 4096
BATCH = 4
SEQ = 2048
DEPTH = 4
DEC_BATCH = 128
DEC_SEQ = 8
PAST_LEN = 8192
PAGE_SIZE = 128

N_HEADS = 16
QK_NOPE_DIM = 128
QK_ROPE_DIM = 64
V_HEAD_DIM = 128
Q_LORA_RANK = 768
KV_LORA_RANK = 256
ATTN_WIDTH = N_HEADS * V_HEAD_DIM
CONV_DIM = D_MODEL - ATTN_WIDTH
CONV_GROUPS = 16
CONV_WIDTH = 3
MIX_WIDTH = ATTN_WIDTH + CONV_DIM
IN_SPLITS = [Q_LORA_RANK,
             Q_LORA_RANK + KV_LORA_RANK,
             Q_LORA_RANK + KV_LORA_RANK + QK_ROPE_DIM,
             Q_LORA_RANK + KV_LORA_RANK + QK_ROPE_DIM + CONV_DIM,
             Q_LORA_RANK + KV_LORA_RANK + QK_ROPE_DIM + 2 * CONV_DIM]
IN_COLS = Q_LORA_RANK + KV_LORA_RANK + QK_ROPE_DIM + 3 * CONV_DIM
D_FF = 11008
N_EXPERTS = 8
TOP_K = 2
D_FF_EXPERT = 5632
N_DENSE = (DEPTH + 1) // 2
N_MOE = DEPTH // 2
PLE_DIM = 256
ROPE_THETA = 10000.0
RMS_EPS = 1e-6
Q_BLOCK = 128
SM_SCALE = (QK_NOPE_DIM + QK_ROPE_DIM) ** -0.5
POOL_NUM, POOL_DEN = 5, 4

kernel_name = "hybrid_mla_shortconv_decoder_step"


def rms_norm(x, g):
    xf = x.astype(jnp.float32)
    y = xf * lax.rsqrt(jnp.mean(xf * xf, axis=-1, keepdims=True) + RMS_EPS)
    return (y * g.astype(jnp.float32)).astype(x.dtype)


def apply_rope(x, pos):
    half = x.shape[-1] // 2
    inv_freq = jnp.exp(-math.log(ROPE_THETA) * jnp.arange(half, dtype=jnp.float32) / half)
    ang = pos.astype(jnp.float32)[:, None] * inv_freq[None, :]
    cos = jnp.cos(ang)[None, :, None, :]
    sin = jnp.sin(ang)[None, :, None, :]
    xf = x.astype(jnp.float32)
    x1, x2 = xf[..., :half], xf[..., half:]
    return jnp.concatenate([x1 * cos - x2 * sin, x2 * cos + x1 * sin], axis=-1).astype(x.dtype)


def mixer_project(a, pos, w_in_l, g_qa_l, w_uq_l, g_qn_l, g_kva_l, g_kn_l, w_uk_l):
    z = a @ w_in_l
    q_a, c_raw, kr_raw, gate_b, gate_c, xc = jnp.split(z, IN_SPLITS, axis=-1)
    q = jnp.einsum('bsl,lhd->bshd', rms_norm(q_a, g_qa_l), w_uq_l)
    q = rms_norm(q, g_qn_l)
    q_lat = jnp.einsum('bshn,rhn->bshr', q[..., :QK_NOPE_DIM], w_uk_l)
    q_rope = apply_rope(q[..., QK_NOPE_DIM:], pos)
    c = rms_norm(c_raw, g_kva_l)
    kr = apply_rope(rms_norm(kr_raw, g_kn_l)[:, :, None, :], pos)[:, :, 0, :]
    return q_lat, q_rope, c, kr, gate_b, gate_c * xc


def mla_prompt(q_lat, q_rope, c, kr):
    b, s = c.shape[0], c.shape[1]
    nb = s // Q_BLOCK
    ql = q_lat.reshape(b, nb, Q_BLOCK, N_HEADS, KV_LORA_RANK).swapaxes(0, 1)
    qr = q_rope.reshape(b, nb, Q_BLOCK, N_HEADS, QK_ROPE_DIM).swapaxes(0, 1)
    kpos = jnp.arange(s)

    def block(args):
        qlb, qrb, start = args
        sc = (jnp.einsum('bqhr,bkr->bhqk', qlb, c, preferred_element_type=jnp.float32)
              + jnp.einsum('bqhd,bkd->bhqk', qrb, kr, preferred_element_type=jnp.float32)) * SM_SCALE
        qpos = start + jnp.arange(Q_BLOCK)
        mask = kpos[None, :] <= qpos[:, None]
        p = jax.nn.softmax(jnp.where(mask, sc, -jnp.inf), axis=-1)
        return jnp.einsum('bhqk,bkr->bqhr', p, c, preferred_element_type=jnp.float32).astype(c.dtype)

    o = lax.map(block, (ql, qr, jnp.arange(nb) * Q_BLOCK))
    return o.swapaxes(0, 1).reshape(b, s, N_HEADS, KV_LORA_RANK)


def mla_sample(q_lat, q_rope, c_new, kr_new, ckv_pool, kr_pool, page_table, layer):
    q = c_new.shape[1]
    s0 = (jnp.einsum('bqhr,bkr->bhqk', q_lat, c_new, preferred_element_type=jnp.float32)
          + jnp.einsum('bqhd,bkd->bhqk', q_rope, kr_new, preferred_element_type=jnp.float32)) * SM_SCALE
    causal = jnp.tril(jnp.ones((q, q), dtype=bool))
    s0 = jnp.where(causal, s0, -jnp.inf)
    m0 = jnp.max(s0, axis=-1)
    p0 = jnp.exp(s0 - m0[..., None])
    l0 = jnp.sum(p0, axis=-1)
    acc0 = jnp.einsum('bhqk,bkr->bhqr', p0, c_new, preferred_element_type=jnp.float32)

    def step(carry, phys):
        m, l, acc = carry
        c = ckv_pool[layer, phys]
        k = kr_pool[layer, phys]
        sc = (jnp.einsum('bqhr,bkr->bhqk', q_lat, c, preferred_element_type=jnp.float32)
              + jnp.einsum('bqhd,bkd->bhqk', q_rope, k, preferred_element_type=jnp.float32)) * SM_SCALE
        m_new = jnp.maximum(m, jnp.max(sc, axis=-1))
        corr = jnp.exp(m - m_new)
        p = jnp.exp(sc - m_new[..., None])
        acc = acc * corr[..., None] + jnp.einsum('bhqk,bkr->bhqr', p, c, preferred_element_type=jnp.float32)
        return (m_new, l * corr + jnp.sum(p, axis=-1), acc), None

    (m, l, acc), _ = lax.scan(step, (m0, l0, acc0), page_table.T)
    o = acc / l[..., None]
    return o.transpose(0, 2, 1, 3).astype(c_new.dtype)


def short_conv(u, prev, w):
    s = u.shape[1]
    full = jnp.concatenate([prev, u], axis=1)
    y = full[:, 0:s] * w[0]
    for k in range(1, CONV_WIDTH):
        y = y + full[:, k:k + s] * w[k]
    return y, full[:, s:]


def swiglu(x, wg, wu, wd):
    return (jax.nn.silu(x @ wg) * (x @ wu)) @ wd


def moe_swiglu(x, w_router, wg, wu, wd):
    logits = (x @ w_router).astype(jnp.float32)
    top_val, top_idx = lax.top_k(logits, TOP_K)
    wts = jax.nn.softmax(top_val, axis=-1)
    gates = jnp.sum(jax.nn.one_hot(top_idx, N_EXPERTS, dtype=jnp.float32) * wts[..., None], axis=-2)
    out = jnp.zeros_like(x)
    for e in range(N_EXPERTS):
        out = out + gates[..., e:e + 1].astype(x.dtype) * swiglu(x, wg[e], wu[e], wd[e])
    return out


def setup_inputs(seed: int = 0) -> dict:
    key = jax.random.key(seed)
    ks = iter(jax.random.split(key, 48))
    f32 = jnp.float32

    def nrm(shape, scale):
        return jax.random.normal(next(ks), shape, f32) * scale

    def gain(shape):
        return 1.0 + 0.05 * jax.random.normal(next(ks), shape, f32)

    n_pages = PAST_LEN // PAGE_SIZE
    n_pool = (DEC_BATCH * n_pages * POOL_NUM) // POOL_DEN
    page_table = jax.random.permutation(next(ks), n_pool)[:DEC_BATCH * n_pages]
    page_table = page_table.reshape(DEC_BATCH, n_pages).astype(jnp.int32)
    return {
        'x_prompt': nrm((BATCH, SEQ, D_MODEL), 1.0),
        'x_sample': nrm((DEC_BATCH, DEC_SEQ, D_MODEL), 1.0),
        'p_prompt': nrm((DEPTH, BATCH, SEQ, PLE_DIM), 1.0),
        'p_sample': nrm((DEPTH, DEC_BATCH, DEC_SEQ, PLE_DIM), 1.0),
        'cache_ckv': nrm((DEPTH, n_pool, PAGE_SIZE, KV_LORA_RANK), 1.0),
        'cache_krope': nrm((DEPTH, n_pool, PAGE_SIZE, QK_ROPE_DIM), 1.0),
        'state_conv': nrm((DEPTH, DEC_BATCH, CONV_WIDTH - 1, CONV_DIM), 1.0),
        'page_table': page_table,
        'g_mix': gain((DEPTH, D_MODEL)),
        'w_in': nrm((DEPTH, D_MODEL, IN_COLS), D_MODEL ** -0.5),
        'g_qa': gain((DEPTH, Q_LORA_RANK)),
        'w_uq': nrm((DEPTH, Q_LORA_RANK, N_HEADS, QK_NOPE_DIM + QK_ROPE_DIM), Q_LORA_RANK ** -0.5),
        'g_qn': gain((DEPTH, QK_NOPE_DIM + QK_ROPE_DIM)),
        'g_kva': gain((DEPTH, KV_LORA_RANK)),
        'g_kn': gain((DEPTH, QK_ROPE_DIM)),
        'w_uk': nrm((DEPTH, KV_LORA_RANK, N_HEADS, QK_NOPE_DIM), KV_LORA_RANK ** -0.5),
        'w_uv': nrm((DEPTH, KV_LORA_RANK, N_HEADS, V_HEAD_DIM), KV_LORA_RANK ** -0.5),
        'conv_w': nrm((DEPTH, CONV_WIDTH, CONV_DIM), CONV_WIDTH ** -0.5),
        'g_ao': gain((DEPTH, N_HEADS, V_HEAD_DIM)),
        'g_co': gain((DEPTH, CONV_GROUPS, CONV_DIM // CONV_GROUPS)),
        'w_o': nrm((DEPTH, MIX_WIDTH, D_MODEL), MIX_WIDTH ** -0.5),
        'g_ffn': gain((DEPTH, D_MODEL)),
        'w_ffn_gate': nrm((N_DENSE, D_MODEL, D_FF), D_MODEL ** -0.5),
        'w_ffn_up': nrm((N_DENSE, D_MODEL, D_FF), D_MODEL ** -0.5),
        'w_ffn_down': nrm((N_DENSE, D_FF, D_MODEL), D_FF ** -0.5),
        'w_router': nrm((N_MOE, D_MODEL, N_EXPERTS), D_MODEL ** -0.5),
        'moe_gate': nrm((N_MOE, N_EXPERTS, D_MODEL, D_FF_EXPERT), D_MODEL ** -0.5),
        'moe_up': nrm((N_MOE, N_EXPERTS, D_MODEL, D_FF_EXPERT), D_MODEL ** -0.5),
        'moe_down': nrm((N_MOE, N_EXPERTS, D_FF_EXPERT, D_MODEL), D_FF_EXPERT ** -0.5),
        'w_pe': nrm((DEPTH, PLE_DIM, D_MODEL), PLE_DIM ** -0.5),
        'g_pe': gain((DEPTH, D_MODEL)),
        'g_pg': gain((DEPTH, D_MODEL)),
        'w_pg': nrm((DEPTH, D_MODEL, D_MODEL), D_MODEL ** -0.5),
    }


def reference(x_prompt, x_sample, p_prompt, p_sample, cache_ckv, cache_krope, state_conv, page_table,
              g_mix, w_in, g_qa, w_uq, g_qn, g_kva, g_kn, w_uk, w_uv, conv_w, g_ao, g_co, w_o,
              g_ffn, w_ffn_gate, w_ffn_up, w_ffn_down, w_router, moe_gate, moe_up, moe_down,
              w_pe, g_pe, g_pg, w_pg):
    past_len = page_table.shape[1] * PAGE_SIZE
    pos_prompt = jnp.arange(x_prompt.shape[1])
    pos_sample = past_len + jnp.arange(x_sample.shape[1])

    def layer(i, h, p_l, pos, conv_prev, attend):
        a = rms_norm(h, g_mix[i])
        q_lat, q_rope, c, kr, gate_b, u = mixer_project(
            a, pos, w_in[i], g_qa[i], w_uq[i], g_qn[i], g_kva[i], g_kn[i], w_uk[i])
        o_lat = attend(q_lat, q_rope, c, kr)
        o = jnp.einsum('bshr,rhv->bshv', o_lat, w_uv[i])
        o = rms_norm(o, g_ao[i]).reshape(o.shape[0], o.shape[1], ATTN_WIDTH)
        y_c, conv_new = short_conv(u, conv_prev, conv_w[i])
        y_c = gate_b * y_c
        y_c = rms_norm(y_c.reshape(y_c.shape[0], y_c.shape[1], CONV_GROUPS, CONV_DIM // CONV_GROUPS),
                       g_co[i]).reshape(y_c.shape)
        h = h + jnp.concatenate([o, y_c], axis=-1) @ w_o[i]
        f = rms_norm(h, g_ffn[i])
        j = i // 2
        if i % 2 == 0:
            h = h + swiglu(f, w_ffn_gate[j], w_ffn_up[j], w_ffn_down[j])
        else:
            h = h + moe_swiglu(f, w_router[j], moe_gate[j], moe_up[j], moe_down[j])
        gate = jax.nn.sigmoid(rms_norm(h, g_pg[i]) @ w_pg[i])
        h = h + gate * rms_norm(p_l @ w_pe[i], g_pe[i])
        return h, c, kr, conv_new

    h_p, h_s = x_prompt, x_sample
    conv_zero = jnp.zeros((x_prompt.shape[0], CONV_WIDTH - 1, CONV_DIM), x_prompt.dtype)
    ckv_p, kr_p, cv_p, ckv_s, kr_s, cv_s = [], [], [], [], [], []
    for i in range(DEPTH):
        h_p, c1, k1, v1 = layer(i, h_p, p_prompt[i], pos_prompt, conv_zero, mla_prompt)
        attend_s = functools.partial(mla_sample, ckv_pool=cache_ckv, kr_pool=cache_krope,
                                     page_table=page_table, layer=i)
        h_s, c2, k2, v2 = layer(i, h_s, p_sample[i], pos_sample, state_conv[i], attend_s)
        ckv_p.append(c1); kr_p.append(k1); cv_p.append(v1)
        ckv_s.append(c2); kr_s.append(k2); cv_s.append(v2)

    ckv_prompt = jnp.stack(ckv_p)
    krope_prompt = jnp.stack(kr_p)
    conv_prompt = jnp.stack(cv_p)
    ckv_sample = jnp.stack(ckv_s)
    krope_sample = jnp.stack(kr_s)
    conv_sample = jnp.stack(cv_s)
    return (h_p, h_s, ckv_prompt, krope_prompt, conv_prompt, ckv_sample, krope_sample, conv_sample)
```

```python
import functools
import math

import jax
import jax.numpy as jnp
from jax import lax
from jax.experimental import pallas as pl
from jax.experimental.pallas import tpu as pltpu

F32 = jnp.float32
BF16 = jnp.bfloat16
RMS_EPS = 1e-6
ROPE_THETA = 10000.0
TOP_K = 2

V7X_VMEM_BYTES = 64 * 1024 * 1024
VMEM_BUDGET = V7X_VMEM_BYTES - 8 * 1024 * 1024
LANE = 128
SUBLANE = 8

NOPE = 128
ROPE = 64
QH = NOPE + 2 * ROPE
KVR = 256
QW = KVR + LANE
VH = 128


def _cparams(sem, vmem_bytes):
    return pltpu.CompilerParams(dimension_semantics=sem,
                                vmem_limit_bytes=int(min(max(vmem_bytes, 16 << 20), VMEM_BUDGET)))


def _tile(n, pref, align=LANE):
    if n <= pref:
        return n
    t = (pref // align) * align
    while t >= align:
        if n % t == 0:
            return t
        t -= align
    return n


def _dot(a, b):
    return jnp.dot(a, b, preferred_element_type=F32)


def _dot_nt(a, b):
    return lax.dot_general(a, b, (((1,), (1,)), ((), ())), preferred_element_type=F32)


def _rms(x, g):
    ms = jnp.mean(x * x, axis=-1, keepdims=True)
    return x * lax.rsqrt(ms + RMS_EPS) * g


def _sigmoid(x):
    return 1.0 / (1.0 + jnp.exp(-x))


def _norm_body(*refs, n_add, router, y_dtype):
    x_ref = refs[0]
    add_refs = refs[1:1 + n_add]
    g_ref = refs[1 + n_add]
    pos = 2 + n_add
    wr_ref = refs[pos] if router else None
    pos += 1 if router else 0
    outs = refs[pos:]
    x = x_ref[...]
    for a in add_refs:
        x = x + a[...]
    oi = 0
    if n_add:
        outs[oi][...] = x
        oi += 1
    y = _rms(x, g_ref[...])
    outs[oi][...] = y.astype(y_dtype)
    oi += 1
    if router:
        outs[oi][...] = jnp.dot(y, wr_ref[...], preferred_element_type=F32,
                                precision=lax.Precision.HIGHEST)


def rmsnorm(x, g, adds=(), router_w=None, y_dtype=BF16):
    t, d = x.shape
    tm = _tile(t, 256, SUBLANE)
    n_add = len(adds)
    row = pl.BlockSpec((tm, d), lambda i: (i, 0))
    in_specs = [row] * (1 + n_add) + [pl.BlockSpec((1, d), lambda i: (0, 0))]
    args = [x, *adds, g.reshape(1, d)]
    out_shape, out_specs = [], []
    if n_add:
        out_shape.append(jax.ShapeDtypeStruct((t, d), F32))
        out_specs.append(row)
    out_shape.append(jax.ShapeDtypeStruct((t, d), y_dtype))
    out_specs.append(row)
    if router_w is not None:
        ne = router_w.shape[1]
        wr = jnp.zeros((d, LANE), F32).at[:, :ne].set(router_w)
        in_specs.append(pl.BlockSpec((d, LANE), lambda i: (0, 0)))
        args.append(wr)
        out_shape.append(jax.ShapeDtypeStruct((t, LANE), F32))
        out_specs.append(pl.BlockSpec((tm, LANE), lambda i: (i, 0)))
    vmem = 2 * tm * d * 4 * (2 + 2 * n_add) + (4 << 20)
    outs = pl.pallas_call(
        functools.partial(_norm_body, n_add=n_add, router=router_w is not None, y_dtype=y_dtype),
        grid=(t // tm,), in_specs=in_specs, out_specs=out_specs, out_shape=out_shape,
        compiler_params=_cparams(("parallel",), vmem), name="rmsnorm")(*args)
    return outs


def _w_spec(w, layer_idx, k, tn, kblock=0):
    lead = tuple(layer_idx)
    nlead = len(lead)
    assert w.ndim == nlead + 2
    return pl.BlockSpec((None,) * nlead + (k, tn), lambda j, i: lead + (kblock, j))


def _mm_plain_body(x_ref, w_ref, o_ref):
    o_ref[...] = _dot(x_ref[...], w_ref[...].astype(BF16)).astype(o_ref.dtype)


def mm_plain(x, w, layer_idx, out_dtype, tm_pref=1024, tn_pref=512):
    m, k = x.shape
    n = w.shape[-1]
    tm, tn = _tile(m, tm_pref, SUBLANE), _tile(n, tn_pref)
    wb = w.dtype.itemsize
    vmem = 2 * (tm * k * 2 + k * tn * wb + tm * tn * 4) + k * tn * 2 + tm * tn * 4 + (4 << 20)
    return pl.pallas_call(
        _mm_plain_body, grid=(n // tn, m // tm),
        in_specs=[pl.BlockSpec((tm, k), lambda j, i: (i, 0)), _w_spec(w, layer_idx, k, tn)],
        out_specs=pl.BlockSpec((tm, tn), lambda j, i: (i, j)),
        out_shape=jax.ShapeDtypeStruct((m, n), out_dtype),
        compiler_params=_cparams(("parallel", "arbitrary"), vmem), name="mm_plain")(x, w)


def _mm_swiglu_body(x_ref, wg_ref, wu_ref, o_ref):
    x = x_ref[...]
    g = _dot(x, wg_ref[...].astype(BF16))
    u = _dot(x, wu_ref[...].astype(BF16))
    o_ref[...] = (g * _sigmoid(g) * u).astype(o_ref.dtype)


def mm_swiglu(x, wg, wu, layer_idx, tm_pref=1024, tn_pref=256):
    m, k = x.shape
    n = wg.shape[-1]
    tm, tn = _tile(m, tm_pref, SUBLANE), _tile(n, tn_pref)
    vmem = 2 * (tm * k * 2 + 2 * k * tn * 4 + tm * tn * 2) + 2 * k * tn * 2 + 3 * tm * tn * 4 + (4 << 20)
    return pl.pallas_call(
        _mm_swiglu_body, grid=(n // tn, m // tm),
        in_specs=[pl.BlockSpec((tm, k), lambda j, i: (i, 0)),
                  _w_spec(wg, layer_idx, k, tn), _w_spec(wu, layer_idx, k, tn)],
        out_specs=pl.BlockSpec((tm, tn), lambda j, i: (i, j)),
        out_shape=jax.ShapeDtypeStruct((m, n), BF16),
        compiler_params=_cparams(("parallel", "arbitrary"), vmem), name="mm_swiglu")(x, wg, wu)


def _mm2_res_body(x1_ref, x2_ref, w1_ref, w2_ref, r_ref, o_ref):
    acc = _dot(x1_ref[...], w1_ref[...].astype(BF16)) + _dot(x2_ref[...], w2_ref[...].astype(BF16))
    o_ref[...] = r_ref[...] + acc


def mm2_res(x1, x2, w, layer_idx, res, tm_pref=1024, tn_pref=512):
    m, k1 = x1.shape
    k2 = x2.shape[1]
    assert k1 == k2
    n = w.shape[-1]
    tm, tn = _tile(m, tm_pref, SUBLANE), _tile(n, tn_pref)
    vmem = 2 * (2 * tm * k1 * 2 + 2 * k1 * tn * 4 + 2 * tm * tn * 4) + 2 * k1 * tn * 2 + tm * tn * 4 + (4 << 20)
    return pl.pallas_call(
        _mm2_res_body, grid=(n // tn, m // tm),
        in_specs=[pl.BlockSpec((tm, k1), lambda j, i: (i, 0)), pl.BlockSpec((tm, k2), lambda j, i: (i, 0)),
                  _w_spec(w, layer_idx, k1, tn, 0), _w_spec(w, layer_idx, k2, tn, 1),
                  pl.BlockSpec((tm, tn), lambda j, i: (i, j))],
        out_specs=pl.BlockSpec((tm, tn), lambda j, i: (i, j)),
        out_shape=jax.ShapeDtypeStruct((m, n), F32),
        compiler_params=_cparams(("parallel", "arbitrary"), vmem), name="mm_wo")(x1, x2, w, w, res)


def _mm_ple_body(x_ref, w_ref, r_ref, e_ref, o_ref):
    acc = _dot(x_ref[...], w_ref[...].astype(BF16))
    o_ref[...] = r_ref[...] + _sigmoid(acc) * e_ref[...].astype(F32)


def mm_ple(x, w, layer_idx, res, e, tm_pref=1024, tn_pref=512):
    m, k = x.shape
    n = w.shape[-1]
    tm, tn = _tile(m, tm_pref, SUBLANE), _tile(n, tn_pref)
    vmem = 2 * (tm * k * 2 + k * tn * 4 + 3 * tm * tn * 4) + k * tn * 2 + 2 * tm * tn * 4 + (4 << 20)
    return pl.pallas_call(
        _mm_ple_body, grid=(n // tn, m // tm),
        in_specs=[pl.BlockSpec((tm, k), lambda j, i: (i, 0)), _w_spec(w, layer_idx, k, tn),
                  pl.BlockSpec((tm, tn), lambda j, i: (i, j)), pl.BlockSpec((tm, tn), lambda j, i: (i, j))],
        out_specs=pl.BlockSpec((tm, tn), lambda j, i: (i, j)),
        out_shape=jax.ShapeDtypeStruct((m, n), F32),
        compiler_params=_cparams(("parallel", "arbitrary"), vmem), name="mm_ple")(x, w, res, e)


def _mm_res_body(x_ref, w_ref, r_ref, o_ref):
    o_ref[...] = r_ref[...] + _dot(x_ref[...], w_ref[...].astype(BF16))


def mm_down(x, w, layer_idx, res, tm_pref=512, tn_pref=512, tk_pref=5632):
    m, k = x.shape
    n = w.shape[-1]
    tm, tn, tk = _tile(m, tm_pref, SUBLANE), _tile(n, tn_pref), _tile(k, tk_pref)
    vmem = 2 * (tm * tk * 2 + tk * tn * 4 + 2 * tm * tn * 4) + tk * tn * 2 + tm * tn * 4 + (4 << 20)
    for kb in range(k // tk):
        res = pl.pallas_call(
            _mm_res_body, grid=(n // tn, m // tm),
            in_specs=[pl.BlockSpec((tm, tk), lambda j, i, kb=kb: (i, kb)), _w_spec(w, layer_idx, tk, tn, kb),
                      pl.BlockSpec((tm, tn), lambda j, i: (i, j))],
            out_specs=pl.BlockSpec((tm, tn), lambda j, i: (i, j)),
            out_shape=jax.ShapeDtypeStruct((m, n), F32),
            compiler_params=_cparams(("parallel", "arbitrary"), vmem), name="mm_down")(x, w, res)
    return res


def _ple_embed_body(p_ref, w_ref, g_ref, o_ref):
    e = _dot(p_ref[...].astype(BF16), w_ref[...].astype(BF16))
    o_ref[...] = _rms(e, g_ref[...]).astype(o_ref.dtype)


def ple_embed(p, w_pe, layer, g):
    t, kp = p.shape
    d = w_pe.shape[-1]
    tm = _tile(t, 512, SUBLANE)
    vmem = 2 * (tm * kp * 4 + kp * d * 4 + tm * d * 2) + 2 * tm * d * 4 + kp * d * 2 + (4 << 20)
    return pl.pallas_call(
        _ple_embed_body, grid=(t // tm,),
        in_specs=[pl.BlockSpec((tm, kp), lambda i: (i, 0)),
                  pl.BlockSpec((None, kp, d), lambda i: (layer, 0, 0)),
                  pl.BlockSpec((1, d), lambda i: (0, 0))],
        out_specs=pl.BlockSpec((tm, d), lambda i: (i, 0)),
        out_shape=jax.ShapeDtypeStruct((t, d), BF16),
        compiler_params=_cparams(("parallel",), vmem), name="ple_embed")(p, w_pe, g.reshape(1, d))


def _qkv_post_body(z_ref, tab_ref, wuq_ref, wuk_ref, gqa_ref, gq_ref, gkva_ref, gk_ref, *outs,
                   nh, qlr, sample, sm_scale):
    if sample:
        q_ref, ckv_ref, kr_ref = outs
    else:
        q_ref, ckv_ref, kr_ref, kk_ref = outs
    tm = z_ref.shape[0]
    tab = tab_ref[...]
    lane = lax.broadcasted_iota(jnp.int32, (tm, LANE), 1)
    lo = lane < ROPE

    qa = _rms(z_ref[:, 0:qlr], gqa_ref[...]).astype(BF16)
    q = _dot(qa, wuq_ref[...])
    gq = gq_ref[...]
    for h in range(nh):
        qn = q[:, h * QH:h * QH + NOPE]
        qr = q[:, h * QH + NOPE:(h + 1) * QH]
        ssq = jnp.sum(qn * qn, axis=-1, keepdims=True) + jnp.sum(jnp.where(lo, qr * qr, 0.0), axis=-1, keepdims=True)
        sc = lax.rsqrt(ssq * (1.0 / (NOPE + ROPE)) + RMS_EPS)
        qn = (qn * sc * gq[:, 0:NOPE]).astype(BF16)
        qlat = _dot(qn, wuk_ref[h]) * sm_scale
        t = qr * sc * gq[:, NOPE:QH] * tab
        rot = jnp.where(lo, t + pltpu.roll(t, ROPE, 1), 0.0) * sm_scale
        if sample:
            q_ref[:, h, :, 0:KVR] = qlat.reshape(tm // SUBLANE, SUBLANE, KVR)
            q_ref[:, h, :, KVR:QW] = rot.reshape(tm // SUBLANE, SUBLANE, LANE)
        else:
            q_ref[h, :, 0:KVR] = qlat.astype(q_ref.dtype)
            q_ref[h, :, KVR:QW] = rot.astype(q_ref.dtype)

    c = _rms(z_ref[:, qlr:qlr + KVR], gkva_ref[...])
    ckv_ref[...] = c
    k2 = z_ref[:, qlr + KVR:qlr + KVR + LANE]
    ssq = jnp.sum(jnp.where(lo, k2 * k2, 0.0), axis=-1, keepdims=True)
    t = k2 * lax.rsqrt(ssq * (1.0 / ROPE) + RMS_EPS) * gk_ref[...] * tab
    kr = t + pltpu.roll(t, ROPE, 1)
    kr_ref[...] = kr[:, 0:ROPE]
    if not sample:
        kk_ref[:, 0:KVR] = c.astype(BF16)
        kk_ref[:, KVR:QW] = jnp.where(lo, kr, 0.0).astype(BF16)


def qkv_post(z1, row0, rows, tab, tab_period, wuq, wuk_t, gqa, gq, gkva, gk, nh, sample, sm_scale):
    qlr = gqa.shape[-1]
    tm = _tile(rows, 256, 16) if not sample else _tile(rows, 128, SUBLANE)
    assert row0 % tm == 0 and tab_period % tm == 0
    rb0 = row0 // tm
    tb = tab_period // tm
    zw = z1.shape[1]
    in_specs = [pl.BlockSpec((tm, zw), lambda i: (rb0 + i, 0)),
                pl.BlockSpec((tm, LANE), lambda i: (i % tb, 0)),
                pl.BlockSpec(wuq.shape, lambda i: (0, 0)),
                pl.BlockSpec(wuk_t.shape, lambda i: (0, 0, 0)),
                pl.BlockSpec((1, qlr), lambda i: (0, 0)),
                pl.BlockSpec((1, QH), lambda i: (0, 0)),
                pl.BlockSpec((1, KVR), lambda i: (0, 0)),
                pl.BlockSpec((1, LANE), lambda i: (0, 0))]
    if sample:
        q_shape = jax.ShapeDtypeStruct((rows // SUBLANE, nh, SUBLANE, QW), F32)
        q_spec = pl.BlockSpec((tm // SUBLANE, nh, SUBLANE, QW), lambda i: (i, 0, 0, 0))
    else:
        q_shape = jax.ShapeDtypeStruct((nh, rows, QW), BF16)
        q_spec = pl.BlockSpec((nh, tm, QW), lambda i: (0, i, 0))
    out_shape = [q_shape, jax.ShapeDtypeStruct((rows, KVR), F32), jax.ShapeDtypeStruct((rows, ROPE), F32)]
    out_specs = [q_spec, pl.BlockSpec((tm, KVR), lambda i: (i, 0)), pl.BlockSpec((tm, ROPE), lambda i: (i, 0))]
    if not sample:
        out_shape.append(jax.ShapeDtypeStruct((rows, QW), BF16))
        out_specs.append(pl.BlockSpec((tm, QW), lambda i: (i, 0)))
    vmem = 2 * (tm * zw * 4 + wuq.size * 2 + wuk_t.size * 2 + nh * tm * QW * 4) + 3 * tm * nh * QH * 4 + (8 << 20)
    return pl.pallas_call(
        functools.partial(_qkv_post_body, nh=nh, qlr=qlr, sample=sample, sm_scale=sm_scale),
        grid=(rows // tm,), in_specs=in_specs, out_specs=out_specs, out_shape=out_shape,
        compiler_params=_cparams(("parallel",), vmem), name="qkv_post_s" if sample else "qkv_post_p",
    )(z1, tab, wuq, wuk_t, gqa.reshape(1, qlr), gq, gkva.reshape(1, KVR), gk)


def _attn_p_body(q_ref, kk_ref, o_ref, m_sc, l_sc, acc_sc, *, nh, tq, tk):
    qi = pl.program_id(1)
    ki = pl.program_id(2)
    last_k = (qi * tq + tq - 1) // tk
    rows = nh * tq

    @pl.when(ki == 0)
    def _():
        m_sc[...] = jnp.full(m_sc.shape, -jnp.inf, F32)
        l_sc[...] = jnp.zeros(l_sc.shape, F32)
        acc_sc[...] = jnp.zeros(acc_sc.shape, F32)

    @pl.when(ki <= last_k)
    def _():
        q = q_ref[...].reshape(rows, QW)
        k = kk_ref[...]
        s = _dot_nt(q, k)
        qpos = (lax.broadcasted_iota(jnp.int32, (rows, tk), 0) & (tq - 1)) + qi * tq
        kpos = lax.broadcasted_iota(jnp.int32, (rows, tk), 1) + ki * tk
        s = jnp.where(kpos <= qpos, s, -jnp.inf)
        m_prev = m_sc[...]
        m_new = jnp.maximum(m_prev, jnp.max(s, axis=-1, keepdims=True))
        alpha = jnp.exp(m_prev - m_new)
        p = jnp.exp(s - m_new)
        l_sc[...] = alpha * l_sc[...] + jnp.sum(p, axis=-1, keepdims=True)
        acc_sc[...] = alpha * acc_sc[...] + _dot(p.astype(BF16), k[:, 0:KVR])
        m_sc[...] = m_new

    @pl.when(ki == last_k)
    def _():
        o = acc_sc[...] / l_sc[...]
        o_ref[...] = o.reshape(nh, tq, KVR).astype(o_ref.dtype)


def attn_prompt(q, kk, b, s, nh):
    tq = _tile(s, 128, 16)
    tk = _tile(s, 512, 16)
    assert tq & (tq - 1) == 0
    nq, nk = s // tq, s // tk
    rows = nh * tq

    def kk_map(bi, qi, ki):
        return (bi * nk + jnp.minimum(ki, (qi * tq + tq - 1) // tk), 0)

    vmem = 2 * (rows * QW * 2 + tk * QW * 2 + rows * KVR * 2) + rows * (KVR + 2 * LANE) * 4 \
        + 4 * rows * tk * 4 + (8 << 20)
    return pl.pallas_call(
        functools.partial(_attn_p_body, nh=nh, tq=tq, tk=tk),
        grid=(b, nq, nk),
        in_specs=[pl.BlockSpec((nh, tq, QW), lambda bi, qi, ki: (0, bi * nq + qi, 0)),
                  pl.BlockSpec((tk, QW), kk_map)],
        out_specs=pl.BlockSpec((None, nh, tq, KVR), lambda bi, qi, ki: (bi, 0, qi, 0)),
        out_shape=jax.ShapeDtypeStruct((b, nh, s, KVR), BF16),
        scratch_shapes=[pltpu.VMEM((rows, 1), F32), pltpu.VMEM((rows, 1), F32), pltpu.VMEM((rows, KVR), F32)],
        compiler_params=_cparams(("parallel", "parallel", "arbitrary"), vmem), name="attn_prompt")(q, kk)


def _attn_s_body(pt_ref, q_ref, cn_ref, kn_ref, *rest, nh, ds, gp, page):
    c_refs = rest[0:gp]
    k_refs = rest[gp:2 * gp]
    o_ref, m_sc, l_sc, acc_sc = rest[2 * gp:]
    g = pl.program_id(1)
    rows = nh * ds
    q = q_ref[...].reshape(rows, QW).astype(BF16)
    q_lat = q[:, 0:KVR]
    q_rope = q[:, KVR:KVR + ROPE]

    def update(s, v):
        m_prev = m_sc[...]
        m_new = jnp.maximum(m_prev, jnp.max(s, axis=-1, keepdims=True))
        alpha = jnp.exp(m_prev - m_new)
        p = jnp.exp(s - m_new)
        l_sc[...] = alpha * l_sc[...] + jnp.sum(p, axis=-1, keepdims=True)
        acc_sc[...] = alpha * acc_sc[...] + _dot(p.astype(BF16), v)
        m_sc[...] = m_new

    @pl.when(g == 0)
    def _():
        m_sc[...] = jnp.full(m_sc.shape, -jnp.inf, F32)
        l_sc[...] = jnp.zeros(l_sc.shape, F32)
        acc_sc[...] = jnp.zeros(acc_sc.shape, F32)
        pad = LANE - ds
        c0 = jnp.concatenate([cn_ref[...], jnp.zeros((pad, KVR), F32)], axis=0).astype(BF16)
        k0 = jnp.concatenate([kn_ref[...], jnp.zeros((pad, ROPE), F32)], axis=0).astype(BF16)
        s = _dot_nt(q_lat, c0) + _dot_nt(q_rope, k0)
        tpos = lax.broadcasted_iota(jnp.int32, (rows, LANE), 0) & (ds - 1)
        kpos = lax.broadcasted_iota(jnp.int32, (rows, LANE), 1)
        update(jnp.where(kpos <= tpos, s, -jnp.inf), c0)

    c = jnp.concatenate([r[...] for r in c_refs], axis=0).astype(BF16)
    k = jnp.concatenate([r[...] for r in k_refs], axis=0).astype(BF16)
    update(_dot_nt(q_lat, c) + _dot_nt(q_rope, k), c)

    @pl.when(g == pl.num_programs(1) - 1)
    def _():
        o = acc_sc[...] / l_sc[...]
        o_ref[...] = o.reshape(nh, ds, KVR)


def attn_sample(q, c_new, k_new, cache_ckv, cache_krope, page_table, layer, nh):
    db, _, ds, _ = q.shape
    n_pages = page_table.shape[1]
    page = cache_ckv.shape[2]
    assert ds == SUBLANE and ds & (ds - 1) == 0
    gp = 8 if n_pages % 8 == 0 else 1
    ng = n_pages // gp
    rows = nh * ds
    pt = page_table.reshape(-1)

    def page_spec(width, i):
        return pl.BlockSpec((None, None, page, width),
                            lambda b, g, pt_ref: (layer, pt_ref[b * n_pages + g * gp + i], 0, 0))

    in_specs = [pl.BlockSpec((None, nh, ds, QW), lambda b, g, pt_ref: (b, 0, 0, 0)),
                pl.BlockSpec((ds, KVR), lambda b, g, pt_ref: (b, 0)),
                pl.BlockSpec((ds, ROPE), lambda b, g, pt_ref: (b, 0))]
    in_specs += [page_spec(KVR, i) for i in range(gp)] + [page_spec(ROPE, i) for i in range(gp)]
    grid_spec = pltpu.PrefetchScalarGridSpec(
        num_scalar_prefetch=1, grid=(db, ng), in_specs=in_specs,
        out_specs=pl.BlockSpec((None, nh, ds, KVR), lambda b, g, pt_ref: (b, 0, 0, 0)),
        scratch_shapes=[pltpu.VMEM((rows, 1), F32), pltpu.VMEM((rows, 1), F32), pltpu.VMEM((rows, KVR), F32)])
    vmem = 2 * gp * page * (KVR + LANE) * 4 + 6 * gp * page * (KVR + LANE) * 2 + 4 * rows * gp * page * 4 + (8 << 20)
    return pl.pallas_call(
        functools.partial(_attn_s_body, nh=nh, ds=ds, gp=gp, page=page),
        grid_spec=grid_spec, out_shape=jax.ShapeDtypeStruct((db, nh, ds, KVR), F32),
        compiler_params=_cparams(("parallel", "arbitrary"), vmem), name="attn_sample",
    )(pt, q, c_new, k_new, *([cache_ckv] * gp), *([cache_krope] * gp))


def _ouv_body(o_ref, w_ref, g_ref, out_ref):
    bx, by, _ = o_ref.shape
    o = o_ref[...].reshape(bx * by, KVR).astype(BF16)
    v = _dot(o, w_ref[...])
    out_ref[...] = _rms(v, g_ref[...]).astype(out_ref.dtype)


def ouv(o_lat, wuv_t, g_ao, bx, by):
    x, nh, y, _ = o_lat.shape
    tm = bx * by
    nyb = y // by
    vmem = 2 * (tm * KVR * 4 + KVR * VH * 2 + tm * VH * 2) + 4 * tm * KVR * 4 + (4 << 20)
    return pl.pallas_call(
        _ouv_body, grid=(x // bx, nyb, nh),
        in_specs=[pl.BlockSpec((bx, None, by, KVR), lambda xi, yi, h: (xi, h, yi, 0)),
                  pl.BlockSpec((None, KVR, VH), lambda xi, yi, h: (h, 0, 0)),
                  pl.BlockSpec((None, 1, VH), lambda xi, yi, h: (h, 0, 0))],
        out_specs=pl.BlockSpec((tm, VH), lambda xi, yi, h: (xi * nyb + yi, h)),
        out_shape=jax.ShapeDtypeStruct((x * y, nh * VH), BF16),
        compiler_params=_cparams(("parallel", "parallel", "arbitrary"), vmem), name="ouv")(o_lat, wuv_t, g_ao)


def _conv_core(gb, gc, xc, p0, p1, period, w_ref, g_ref, y_ref, groups):
    tm, c = gb.shape
    u = gc.astype(F32) * xc.astype(F32)
    t = lax.broadcasted_iota(jnp.int32, (tm, c), 0) & (period - 1)
    um1 = jnp.where(t == 0, p1, pltpu.roll(u, 1, 0))
    um2 = jnp.where(t == 0, p0, jnp.where(t == 1, p1, pltpu.roll(u, 2, 0)))
    w = w_ref[...]
    y = um2 * w[0:1, :] + um1 * w[1:2, :] + u * w[2:3, :]
    y = gb.astype(F32) * y
    gw = c // groups
    g = g_ref[...]
    for i in range(groups):
        sl = slice(i * gw, (i + 1) * gw)
        y_ref[:, sl] = _rms(y[:, sl], g[:, sl]).astype(y_ref.dtype)
    return u


def _conv_p_body(gb_ref, gc_ref, xc_ref, w_ref, g_ref, y_ref, st_ref, prev_sc, *, groups):
    ts = gb_ref.shape[0]

    @pl.when(pl.program_id(1) == 0)
    def _():
        prev_sc[...] = jnp.zeros(prev_sc.shape, F32)

    u = _conv_core(gb_ref[...], gc_ref[...], xc_ref[...], prev_sc[0:1, :], prev_sc[1:2, :], ts,
                   w_ref, g_ref, y_ref, groups)
    tail = u[ts - 2:ts, :]
    prev_sc[0:2, :] = tail
    st_ref[...] = tail


def conv_prompt(z2, b, s, conv_w, g_co, groups):
    c = conv_w.shape[-1]
    ts = _tile(s, 256, 16)
    assert ts & (ts - 1) == 0 and conv_w.shape[0] == 3
    ns = s // ts

    def col(j):
        return pl.BlockSpec((ts, c), lambda bi, si: (bi * ns + si, j))

    vmem = 2 * (3 * ts * c * 2 + ts * c * 2) + 8 * ts * c * 4 + (4 << 20)
    return pl.pallas_call(
        functools.partial(_conv_p_body, groups=groups), grid=(b, ns),
        in_specs=[col(0), col(1), col(2), pl.BlockSpec((3, c), lambda bi, si: (0, 0)),
                  pl.BlockSpec((1, c), lambda bi, si: (0, 0))],
        out_specs=[pl.BlockSpec((ts, c), lambda bi, si: (bi * ns + si, 0)),
                   pl.BlockSpec((None, 2, c), lambda bi, si: (bi, 0, 0))],
        out_shape=[jax.ShapeDtypeStruct((b * s, c), BF16), jax.ShapeDtypeStruct((b, 2, c), F32)],
        scratch_shapes=[pltpu.VMEM((SUBLANE, c), F32)],
        compiler_params=_cparams(("parallel", "arbitrary"), vmem), name="conv_prompt",
    )(z2, z2, z2, conv_w, g_co.reshape(1, c))


def _conv_s_body(gb_ref, gc_ref, xc_ref, st_ref, w_ref, g_ref, y_ref, new_ref, *, groups, ds):
    tm, c = gb_ref.shape
    bb = tm // ds
    st = st_ref[...]
    p0 = jnp.broadcast_to(st[:, 0:1, :], (bb, ds, c)).reshape(tm, c)
    p1 = jnp.broadcast_to(st[:, 1:2, :], (bb, ds, c)).reshape(tm, c)
    u = _conv_core(gb_ref[...], gc_ref[...], xc_ref[...], p0, p1, ds, w_ref, g_ref, y_ref, groups)
    new_ref[...] = u.reshape(bb, ds, c)[:, ds - 2:ds, :]


def conv_sample(z2, row0, db, ds, state, conv_w, g_co, groups):
    c = conv_w.shape[-1]
    assert ds == SUBLANE
    bb = _tile(db, 16, 1)
    tm = bb * ds
    assert row0 % tm == 0
    rb0 = row0 // tm

    def col(j):
        return pl.BlockSpec((tm, c), lambda i: (rb0 + i, j))

    vmem = 2 * (3 * tm * c * 2 + tm * c * 2 + 2 * bb * SUBLANE * c * 4) + 10 * tm * c * 4 + (4 << 20)
    return pl.pallas_call(
        functools.partial(_conv_s_body, groups=groups, ds=ds), grid=(db // bb,),
        in_specs=[col(0), col(1), col(2), pl.BlockSpec((bb, 2, c), lambda i: (i, 0, 0)),
                  pl.BlockSpec((3, c), lambda i: (0, 0)), pl.BlockSpec((1, c), lambda i: (0, 0))],
        out_specs=[pl.BlockSpec((tm, c), lambda i: (i, 0)), pl.BlockSpec((bb, 2, c), lambda i: (i, 0, 0))],
        out_shape=[jax.ShapeDtypeStruct((db * ds, c), BF16), jax.ShapeDtypeStruct((db, 2, c), F32)],
        compiler_params=_cparams(("parallel",), vmem), name="conv_sample",
    )(z2, z2, z2, state, conv_w, g_co.reshape(1, c))


def _gather_body(idx_ref, valid_ref, src_ref, o_ref, buf, sem, *, rows):
    i = pl.program_id(0)

    def row_copy(r, src_row):
        return pltpu.make_async_copy(src_ref.at[pl.ds(src_row, 1)], buf.at[pl.ds(r, 1)], sem)

    @pl.when(valid_ref[i] != 0)
    def _():
        def issue(r, carry):
            row_copy(r, idx_ref[i * rows + r]).start()
            return carry

        lax.fori_loop(0, rows, issue, 0)

        def wait(r, carry):
            row_copy(r, 0).wait()
            return carry

        lax.fori_loop(0, rows, wait, 0)
        o_ref[...] = buf[...].astype(o_ref.dtype)

    @pl.when(valid_ref[i] == 0)
    def _():
        o_ref[...] = jnp.zeros(o_ref.shape, o_ref.dtype)


def gather_rows(src, idx, tile_valid, rows):
    p = idx.shape[0]
    d = src.shape[1]
    grid_spec = pltpu.PrefetchScalarGridSpec(
        num_scalar_prefetch=2, grid=(p // rows,),
        in_specs=[pl.BlockSpec(memory_space=pl.ANY)],
        out_specs=pl.BlockSpec((rows, d), lambda i, idx_ref, v_ref: (i, 0)),
        scratch_shapes=[pltpu.VMEM((rows, d), src.dtype), pltpu.SemaphoreType.DMA(())])
    vmem = rows * d * 4 + 2 * rows * d * 2 + rows * d * 4 + (4 << 20)
    return pl.pallas_call(
        functools.partial(_gather_body, rows=rows), grid_spec=grid_spec,
        out_shape=jax.ShapeDtypeStruct((p, d), BF16),
        compiler_params=_cparams(("arbitrary",), vmem), name="moe_gather")(idx, tile_valid, src)


def _combine_body(p0_ref, p1_ref, ys_ref, h_ref, o_ref, b0, b1, sem, *, rows):
    i = pl.program_id(0)

    def row_copy(r, src_row, buf):
        return pltpu.make_async_copy(ys_ref.at[pl.ds(src_row, 1)], buf.at[pl.ds(r, 1)], sem)

    def issue(r, carry):
        row_copy(r, p0_ref[i * rows + r], b0).start()
        row_copy(r, p1_ref[i * rows + r], b1).start()
        return carry

    lax.fori_loop(0, rows, issue, 0)

    def wait(r, carry):
        row_copy(r, 0, b0).wait()
        row_copy(r, 0, b1).wait()
        return carry

    lax.fori_loop(0, rows, wait, 0)
    o_ref[...] = h_ref[...] + b0[...] + b1[...]


def moe_combine(ys, pos0, pos1, h, rows=256):
    t, d = h.shape
    rows = _tile(t, rows, SUBLANE)
    grid_spec = pltpu.PrefetchScalarGridSpec(
        num_scalar_prefetch=2, grid=(t // rows,),
        in_specs=[pl.BlockSpec(memory_space=pl.ANY), pl.BlockSpec((rows, d), lambda i, a, b: (i, 0))],
        out_specs=pl.BlockSpec((rows, d), lambda i, a, b: (i, 0)),
        scratch_shapes=[pltpu.VMEM((rows, d), F32), pltpu.VMEM((rows, d), F32), pltpu.SemaphoreType.DMA(())])
    vmem = 6 * rows * d * 4 + rows * d * 4 + (4 << 20)
    return pl.pallas_call(
        functools.partial(_combine_body, rows=rows), grid_spec=grid_spec,
        out_shape=jax.ShapeDtypeStruct((t, d), F32),
        compiler_params=_cparams(("arbitrary",), vmem), name="moe_combine")(pos0, pos1, ys, h)


def _moe_up_body(te_ref, tb_ref, tv_ref, x_ref, wg_ref, wu_ref, o_ref):
    @pl.when(tv_ref[pl.program_id(1)] != 0)
    def _():
        x = x_ref[...]
        g = _dot(x, wg_ref[...].astype(BF16))
        u = _dot(x, wu_ref[...].astype(BF16))
        o_ref[...] = (g * _sigmoid(g) * u).astype(o_ref.dtype)

    @pl.when(tv_ref[pl.program_id(1)] == 0)
    def _():
        o_ref[...] = jnp.zeros(o_ref.shape, o_ref.dtype)


def moe_up_call(xs, wg, wu, j, tile_expert, tile_block, tile_valid, tm, tf_pref=256):
    p, k = xs.shape
    f = wg.shape[-1]
    tf = _tile(f, tf_pref)

    def w_map(fi, i, te, tb, tv):
        return (j, te[i], 0, fi)

    grid_spec = pltpu.PrefetchScalarGridSpec(
        num_scalar_prefetch=3, grid=(f // tf, p // tm),
        in_specs=[pl.BlockSpec((tm, k), lambda fi, i, te, tb, tv: (tb[i], 0)),
                  pl.BlockSpec((None, None, k, tf), w_map), pl.BlockSpec((None, None, k, tf), w_map)],
        out_specs=pl.BlockSpec((tm, tf), lambda fi, i, te, tb, tv: (i, fi)))
    vmem = 2 * (tm * k * 2 + 2 * k * tf * 4 + tm * tf * 2) + 2 * k * tf * 2 + 3 * tm * tf * 4 + (4 << 20)
    return pl.pallas_call(
        _moe_up_body, grid_spec=grid_spec, out_shape=jax.ShapeDtypeStruct((p, f), BF16),
        compiler_params=_cparams(("parallel", "arbitrary"), vmem), name="moe_up",
    )(tile_expert, tile_block, tile_valid, xs, wg, wu)


def _moe_down_body(te_ref, tb_ref, tv_ref, a_ref, w_ref, g_ref, o_ref):
    @pl.when(tv_ref[pl.program_id(1)] != 0)
    def _():
        o_ref[...] = _dot(a_ref[...], w_ref[...].astype(BF16)) * g_ref[...]

    @pl.when(tv_ref[pl.program_id(1)] == 0)
    def _():
        o_ref[...] = jnp.zeros(o_ref.shape, o_ref.dtype)


def moe_down_call(act, wd, j, row_gate, tile_expert, tile_block, tile_valid, tm, tn_pref=512):
    p, k = act.shape
    n = wd.shape[-1]
    tn = _tile(n, tn_pref)
    grid_spec = pltpu.PrefetchScalarGridSpec(
        num_scalar_prefetch=3, grid=(n // tn, p // tm),
        in_specs=[pl.BlockSpec((tm, k), lambda ni, i, te, tb, tv: (tb[i], 0)),
                  pl.BlockSpec((None, None, k, tn), lambda ni, i, te, tb, tv: (j, te[i], 0, ni)),
                  pl.BlockSpec((tm, 1), lambda ni, i, te, tb, tv: (tb[i], 0))],
        out_specs=pl.BlockSpec((tm, tn), lambda ni, i, te, tb, tv: (i, ni)))
    vmem = 2 * (tm * k * 2 + k * tn * 4 + tm * tn * 4 + tm * LANE * 4) + k * tn * 2 + tm * tn * 4 + (4 << 20)
    return pl.pallas_call(
        _moe_down_body, grid_spec=grid_spec, out_shape=jax.ShapeDtypeStruct((p, n), F32),
        compiler_params=_cparams(("parallel", "arbitrary"), vmem), name="moe_down",
    )(tile_expert, tile_block, tile_valid, act, wd, row_gate)


def moe_route(logits, n_exp, tm):
    t = logits.shape[0]
    top_val, top_idx = lax.top_k(logits[:, :n_exp], TOP_K)
    wts = jax.nn.softmax(top_val, axis=-1)
    e_flat = top_idx.T.reshape(-1).astype(jnp.int32)
    g_flat = wts.T.reshape(-1)
    tok = jnp.tile(jnp.arange(t, dtype=jnp.int32), TOP_K)
    counts = jnp.sum(jax.nn.one_hot(e_flat, n_exp, dtype=jnp.int32), axis=0)
    padded = ((counts + tm - 1) // tm) * tm
    pad_end = jnp.cumsum(padded)
    pad_start = pad_end - padded
    start = jnp.cumsum(counts) - counts
    order = jnp.argsort(e_flat, stable=True)
    e_sorted = e_flat[order]
    dest_sorted = pad_start[e_sorted] + jnp.arange(TOP_K * t, dtype=jnp.int32) - start[e_sorted]
    p = TOP_K * t + n_exp * tm
    row_token = jnp.zeros((p,), jnp.int32).at[dest_sorted].set(tok[order])
    row_gate = jnp.zeros((p,), F32).at[dest_sorted].set(g_flat[order])
    dest = jnp.zeros((TOP_K * t,), jnp.int32).at[order].set(dest_sorted)
    nt = p // tm
    tile_start = jnp.arange(nt, dtype=jnp.int32) * tm
    total = pad_end[-1]
    tile_valid = (tile_start < total).astype(jnp.int32)
    last_tile = total // tm - 1
    tile_block = jnp.minimum(jnp.arange(nt, dtype=jnp.int32), last_tile).astype(jnp.int32)
    tile_expert = jnp.minimum(jnp.searchsorted(pad_end, tile_block * tm, side="right"), n_exp - 1).astype(jnp.int32)
    return row_token, row_gate.reshape(p, 1), dest[:t], dest[t:], tile_expert, tile_block, tile_valid


def _rope_table(pos):
    half = ROPE // 2
    inv_freq = jnp.exp(-math.log(ROPE_THETA) * jnp.arange(half, dtype=F32) / half)
    ang = pos.astype(F32)[:, None] * inv_freq[None, :]
    cos, sin = jnp.cos(ang), jnp.sin(ang)
    return jnp.concatenate([cos, cos, -sin, sin], axis=-1)


def _roll_half(x):
    return jnp.concatenate([x[..., ROPE // 2:], x[..., :ROPE // 2]], axis=-1)


def kernel(x_prompt, x_sample, p_prompt, p_sample, cache_ckv, cache_krope, state_conv, page_table, g_mix, w_in, g_qa, w_uq, g_qn, g_kva, g_kn, w_uk, w_uv, conv_w, g_ao, g_co, w_o, g_ffn, w_ffn_gate, w_ffn_up, w_ffn_down, w_router, moe_gate, moe_up, moe_down, w_pe, g_pe, g_pg, w_pg):
    b, s, d = x_prompt.shape
    db, ds, _ = x_sample.shape
    depth = w_in.shape[0]
    qlr = g_qa.shape[-1]
    nh = w_uq.shape[2]
    c_dim = conv_w.shape[-1]
    groups = g_co.shape[1]
    n_exp = w_router.shape[-1]
    assert w_uq.shape[3] == NOPE + ROPE and g_kva.shape[-1] == KVR and g_kn.shape[-1] == ROPE
    assert w_uv.shape[-1] == VH and cache_ckv.shape[-1] == KVR and cache_krope.shape[-1] == ROPE
    assert c_dim // groups == LANE and nh * VH == c_dim
    tp, tsmp = b * s, db * ds
    past_len = page_table.shape[1] * cache_ckv.shape[2]
    sm_scale = float((NOPE + ROPE) ** -0.5)
    c0 = qlr + KVR + ROPE
    moe_tm = 512

    h = jnp.concatenate([x_prompt.reshape(tp, d), x_sample.reshape(tsmp, d)], axis=0)
    p_all = jnp.concatenate([p_prompt.reshape(depth, tp, -1), p_sample.reshape(depth, tsmp, -1)], axis=1)
    tab_p = _rope_table(jnp.arange(s))
    tab_s = jnp.tile(_rope_table(past_len + jnp.arange(ds)), (LANE // ds, 1))

    outs = [[] for _ in range(6)]
    for i in range(depth):
        w1 = jnp.concatenate([w_in[i, :, :c0], _roll_half(w_in[i, :, c0 - ROPE:c0])], axis=1)
        w2 = w_in[i, :, c0:].astype(BF16)
        wq = w_uq[i]
        wuq = jnp.concatenate([wq, _roll_half(wq[..., NOPE:])], axis=-1).reshape(qlr, nh * QH).astype(BF16)
        wuk_t = jnp.transpose(w_uk[i], (1, 2, 0)).astype(BF16)
        wuv_t = jnp.transpose(w_uv[i], (1, 0, 2)).astype(BF16)
        gq = jnp.concatenate([g_qn[i], _roll_half(g_qn[i, NOPE:])]).reshape(1, QH)
        gk = jnp.concatenate([g_kn[i], _roll_half(g_kn[i])]).reshape(1, LANE)
        g_ao_i = g_ao[i].reshape(nh, 1, VH)

        (a,) = rmsnorm(h, g_mix[i])
        z1 = mm_plain(a, w1, (), F32, tn_pref=384)
        z2 = mm_plain(a, w2, (), BF16)
        q_p, ckv_p, kr_p, kk_p = qkv_post(z1, 0, tp, tab_p, s, wuq, wuk_t, g_qa[i], gq, g_kva[i], gk,
                                          nh, False, sm_scale)
        q_s, ckv_s, kr_s = qkv_post(z1, tp, tsmp, tab_s, tab_s.shape[0], wuq, wuk_t, g_qa[i], gq, g_kva[i], gk,
                                    nh, True, sm_scale)
        ol_p = attn_prompt(q_p, kk_p, b, s, nh)
        ol_s = attn_sample(q_s, ckv_s, kr_s, cache_ckv, cache_krope, page_table, i, nh)
        o_p = ouv(ol_p, wuv_t, g_ao_i, 1, _tile(s, 512, 16))
        o_s = ouv(ol_s, wuv_t, g_ao_i, _tile(db, 64, 1), ds)
        yc_p, cv_p = conv_prompt(z2, b, s, conv_w[i], g_co[i], groups)
        yc_s, cv_s = conv_sample(z2, tp, db, ds, state_conv[i], conv_w[i], g_co[i], groups)
        o_all = jnp.concatenate([o_p, o_s], axis=0)
        yc_all = jnp.concatenate([yc_p, yc_s], axis=0)
        h = mm2_res(o_all, yc_all, w_o, (i,), h)

        j = i // 2
        if i % 2 == 0:
            (f,) = rmsnorm(h, g_ffn[i])
            act = mm_swiglu(f, w_ffn_gate, w_ffn_up, (j,))
            h = mm_down(act, w_ffn_down, (j,), h)
        else:
            f, logits = rmsnorm(h, g_ffn[i], router_w=w_router[j], y_dtype=F32)
            row_token, row_gate, pos0, pos1, t_exp, t_blk, t_val = moe_route(logits, n_exp, moe_tm)
            xs = gather_rows(f, row_token, t_val, moe_tm)
            act = moe_up_call(xs, moe_gate, moe_up, j, t_exp, t_blk, t_val, moe_tm)
            ys = moe_down_call(act, moe_down, j, row_gate, t_exp, t_blk, t_val, moe_tm)
            h = moe_combine(ys, pos0, pos1, h)

        (hn,) = rmsnorm(h, g_pg[i])
        e = ple_embed(p_all[i], w_pe, i, g_pe[i])
        h = mm_ple(hn, w_pg, (i,), h, e)

        for lst, v in zip(outs, (ckv_p, kr_p, cv_p, ckv_s, kr_s, cv_s)):
            lst.append(v)

    ckv_prompt = jnp.stack(outs[0]).reshape(depth, b, s, KVR)
    krope_prompt = jnp.stack(outs[1]).reshape(depth, b, s, ROPE)
    conv_prompt_out = jnp.stack(outs[2])
    ckv_sample = jnp.stack(outs[3]).reshape(depth, db, ds, KVR)
    krope_sample = jnp.stack(outs[4]).reshape(depth, db, ds, ROPE)
    conv_sample_out = jnp.stack(outs[5])
    return (h[:tp].reshape(b, s, d), h[tp:].reshape(db, ds, d), ckv_prompt, krope_prompt, conv_prompt_out,
            ckv_sample, krope_sample, conv_sample_out)
```

```python
import functools
import math

import jax
import jax.numpy as jnp
from jax import lax
from jax.experimental import pallas as pl
from jax.experimental.pallas import tpu as pltpu

F32 = jnp.float32
BF16 = jnp.bfloat16
RMS_EPS = 1e-6
ROPE_THETA = 10000.0
TOP_K = 2
LOG2E = math.log2(math.e)

V7X_VMEM_BYTES = 64 * 1024 * 1024
VMEM_BUDGET = V7X_VMEM_BYTES - 8 * 1024 * 1024
LANE = 128
SUBLANE = 8

NOPE = 128
ROPE = 64
QH = NOPE + 2 * ROPE
KVR = 256
QW = KVR + LANE
VH = 128


def _cparams(sem, vmem_bytes):
    return pltpu.CompilerParams(dimension_semantics=sem,
                                vmem_limit_bytes=int(min(max(vmem_bytes, 16 << 20), VMEM_BUDGET)))


def _tile(n, pref, align=LANE):
    if n <= pref:
        return n
    t = (pref // align) * align
    while t >= align:
        if n % t == 0:
            return t
        t -= align
    return n


def _dot(a, b):
    return jnp.dot(a, b, preferred_element_type=F32)


def _dot_nt(a, b):
    return lax.dot_general(a, b, (((1,), (1,)), ((), ())), preferred_element_type=F32)


def _rms(x, g):
    ms = jnp.mean(x * x, axis=-1, keepdims=True)
    return x * lax.rsqrt(ms + RMS_EPS) * g


def _sigmoid(x):
    return 1.0 / (1.0 + jnp.exp(-x))


def _first_inner_step():
    return pl.program_id(1) == 0


def _norm_body(*refs, n_add, router, y_dtype):
    x_ref = refs[0]
    add_refs = refs[1:1 + n_add]
    g_ref = refs[1 + n_add]
    pos = 2 + n_add
    wr_ref = refs[pos] if router else None
    pos += 1 if router else 0
    outs = refs[pos:]
    x = x_ref[...]
    for a in add_refs:
        x = x + a[...]
    oi = 0
    if n_add:
        outs[oi][...] = x
        oi += 1
    y = _rms(x, g_ref[...])
    outs[oi][...] = y.astype(y_dtype)
    oi += 1
    if router:
        outs[oi][...] = jnp.dot(y, wr_ref[...], preferred_element_type=F32,
                                precision=lax.Precision.HIGHEST)


def rmsnorm(x, g, adds=(), router_w=None, y_dtype=BF16):
    t, d = x.shape
    tm = _tile(t, 256, SUBLANE)
    n_add = len(adds)
    row = pl.BlockSpec((tm, d), lambda i: (i, 0))
    in_specs = [row] * (1 + n_add) + [pl.BlockSpec((1, d), lambda i: (0, 0))]
    args = [x, *adds, g.reshape(1, d)]
    out_shape, out_specs = [], []
    if n_add:
        out_shape.append(jax.ShapeDtypeStruct((t, d), F32))
        out_specs.append(row)
    out_shape.append(jax.ShapeDtypeStruct((t, d), y_dtype))
    out_specs.append(row)
    if router_w is not None:
        ne = router_w.shape[1]
        wr = jnp.zeros((d, LANE), F32).at[:, :ne].set(router_w)
        in_specs.append(pl.BlockSpec((d, LANE), lambda i: (0, 0)))
        args.append(wr)
        out_shape.append(jax.ShapeDtypeStruct((t, LANE), F32))
        out_specs.append(pl.BlockSpec((tm, LANE), lambda i: (i, 0)))
    vmem = 2 * tm * d * 4 * (2 + 2 * n_add) + (4 << 20)
    outs = pl.pallas_call(
        functools.partial(_norm_body, n_add=n_add, router=router_w is not None, y_dtype=y_dtype),
        grid=(t // tm,), in_specs=in_specs, out_specs=out_specs, out_shape=out_shape,
        compiler_params=_cparams(("parallel",), vmem), name="rmsnorm")(*args)
    return outs


def _w_spec(w, layer_idx, k, tn, kblock=0):
    lead = tuple(layer_idx)
    nlead = len(lead)
    assert w.ndim == nlead + 2
    return pl.BlockSpec((None,) * nlead + (k, tn), lambda j, i: lead + (kblock, j))


def _mm_nt_body(x_ref, w_ref, o_ref, wb_sc):
    @pl.when(_first_inner_step())
    def _():
        wb_sc[...] = w_ref[0].astype(BF16)

    o_ref[...] = _dot_nt(x_ref[...], wb_sc[...]).astype(o_ref.dtype)


def mm_nt(x, wt, layer, row0, n, out_dtype, tm_pref=1024, tn_pref=512):
    m, k = x.shape
    tm, tn = _tile(m, tm_pref, SUBLANE), _tile(n, tn_pref)
    assert row0 % SUBLANE == 0
    w_spec = pl.BlockSpec((pl.Element(1), pl.Element(tn), pl.Element(k)), lambda j, i: (layer, pl.multiple_of(row0 + j * tn, SUBLANE), 0))
    vmem = 2 * (tm * k * 2 + k * tn * 4 + tm * tn * 4) + k * tn * 2 + tm * tn * 4 + (4 << 20)
    return pl.pallas_call(
        _mm_nt_body, grid=(n // tn, m // tm),
        in_specs=[pl.BlockSpec((tm, k), lambda j, i: (i, 0)), w_spec],
        out_specs=pl.BlockSpec((tm, tn), lambda j, i: (i, j)),
        out_shape=jax.ShapeDtypeStruct((m, n), out_dtype),
        scratch_shapes=[pltpu.VMEM((tn, k), BF16)],
        compiler_params=_cparams(("parallel", "arbitrary"), vmem), name="mm_in")(x, wt)


def _mm_swiglu_body(x_ref, wg_ref, wu_ref, o_ref, wg_sc, wu_sc):
    @pl.when(_first_inner_step())
    def _():
        wg_sc[...] = wg_ref[...].astype(BF16)
        wu_sc[...] = wu_ref[...].astype(BF16)

    x = x_ref[...]
    g = _dot(x, wg_sc[...])
    u = _dot(x, wu_sc[...])
    o_ref[...] = (g * _sigmoid(g) * u).astype(o_ref.dtype)


def mm_swiglu(x, wg, wu, layer_idx, tm_pref=1024, tn_pref=256):
    m, k = x.shape
    n = wg.shape[-1]
    tm, tn = _tile(m, tm_pref, SUBLANE), _tile(n, tn_pref)
    vmem = 2 * (tm * k * 2 + 2 * k * tn * 4 + tm * tn * 2) + 2 * k * tn * 2 + 3 * tm * tn * 4 + (4 << 20)
    return pl.pallas_call(
        _mm_swiglu_body, grid=(n // tn, m // tm),
        in_specs=[pl.BlockSpec((tm, k), lambda j, i: (i, 0)),
                  _w_spec(wg, layer_idx, k, tn), _w_spec(wu, layer_idx, k, tn)],
        out_specs=pl.BlockSpec((tm, tn), lambda j, i: (i, j)),
        out_shape=jax.ShapeDtypeStruct((m, n), BF16),
        scratch_shapes=[pltpu.VMEM((k, tn), BF16), pltpu.VMEM((k, tn), BF16)],
        compiler_params=_cparams(("parallel", "arbitrary"), vmem), name="mm_swiglu")(x, wg, wu)


def _mm2_res_body(x1_ref, x2_ref, w1_ref, w2_ref, r_ref, o_ref, w1_sc, w2_sc):
    @pl.when(_first_inner_step())
    def _():
        w1_sc[...] = w1_ref[...].astype(BF16)
        w2_sc[...] = w2_ref[...].astype(BF16)

    acc = _dot(x1_ref[...], w1_sc[...]) + _dot(x2_ref[...], w2_sc[...])
    o_ref[...] = r_ref[...] + acc


def mm2_res(x1, x2, w, layer_idx, res, tm_pref=1024, tn_pref=512):
    m, k1 = x1.shape
    k2 = x2.shape[1]
    assert k1 == k2
    n = w.shape[-1]
    tm, tn = _tile(m, tm_pref, SUBLANE), _tile(n, tn_pref)
    vmem = 2 * (2 * tm * k1 * 2 + 2 * k1 * tn * 4 + 2 * tm * tn * 4) + 2 * k1 * tn * 2 + tm * tn * 4 + (4 << 20)
    return pl.pallas_call(
        _mm2_res_body, grid=(n // tn, m // tm),
        in_specs=[pl.BlockSpec((tm, k1), lambda j, i: (i, 0)), pl.BlockSpec((tm, k2), lambda j, i: (i, 0)),
                  _w_spec(w, layer_idx, k1, tn, 0), _w_spec(w, layer_idx, k2, tn, 1),
                  pl.BlockSpec((tm, tn), lambda j, i: (i, j))],
        out_specs=pl.BlockSpec((tm, tn), lambda j, i: (i, j)),
        out_shape=jax.ShapeDtypeStruct((m, n), F32),
        scratch_shapes=[pltpu.VMEM((k1, tn), BF16), pltpu.VMEM((k2, tn), BF16)],
        compiler_params=_cparams(("parallel", "arbitrary"), vmem), name="mm_wo")(x1, x2, w, w, res)


def _mm_ple_body(x_ref, w_ref, r_ref, e_ref, o_ref, w_sc):
    @pl.when(_first_inner_step())
    def _():
        w_sc[...] = w_ref[...].astype(BF16)

    acc = _dot(x_ref[...], w_sc[...])
    o_ref[...] = r_ref[...] + _sigmoid(acc) * e_ref[...].astype(F32)


def mm_ple(x, w, layer_idx, res, e, tm_pref=1024, tn_pref=512):
    m, k = x.shape
    n = w.shape[-1]
    tm, tn = _tile(m, tm_pref, SUBLANE), _tile(n, tn_pref)
    vmem = 2 * (tm * k * 2 + k * tn * 4 + 3 * tm * tn * 4) + k * tn * 2 + 2 * tm * tn * 4 + (4 << 20)
    return pl.pallas_call(
        _mm_ple_body, grid=(n // tn, m // tm),
        in_specs=[pl.BlockSpec((tm, k), lambda j, i: (i, 0)), _w_spec(w, layer_idx, k, tn),
                  pl.BlockSpec((tm, tn), lambda j, i: (i, j)), pl.BlockSpec((tm, tn), lambda j, i: (i, j))],
        out_specs=pl.BlockSpec((tm, tn), lambda j, i: (i, j)),
        out_shape=jax.ShapeDtypeStruct((m, n), F32),
        scratch_shapes=[pltpu.VMEM((k, tn), BF16)],
        compiler_params=_cparams(("parallel", "arbitrary"), vmem), name="mm_ple")(x, w, res, e)


def _mm_res_body(x_ref, w_ref, r_ref, o_ref, w_sc):
    @pl.when(_first_inner_step())
    def _():
        w_sc[...] = w_ref[...].astype(BF16)

    o_ref[...] = r_ref[...] + _dot(x_ref[...], w_sc[...])


def mm_down(x, w, layer_idx, res, tm_pref=512, tn_pref=512, tk_pref=5632):
    m, k = x.shape
    n = w.shape[-1]
    tm, tn, tk = _tile(m, tm_pref, SUBLANE), _tile(n, tn_pref), _tile(k, tk_pref)
    vmem = 2 * (tm * tk * 2 + tk * tn * 4 + 2 * tm * tn * 4) + tk * tn * 2 + tm * tn * 4 + (4 << 20)
    for kb in range(k // tk):
        res = pl.pallas_call(
            _mm_res_body, grid=(n // tn, m // tm),
            in_specs=[pl.BlockSpec((tm, tk), lambda j, i, kb=kb: (i, kb)), _w_spec(w, layer_idx, tk, tn, kb),
                      pl.BlockSpec((tm, tn), lambda j, i: (i, j))],
            out_specs=pl.BlockSpec((tm, tn), lambda j, i: (i, j)),
            out_shape=jax.ShapeDtypeStruct((m, n), F32),
            scratch_shapes=[pltpu.VMEM((tk, tn), BF16)],
            compiler_params=_cparams(("parallel", "arbitrary"), vmem), name="mm_down")(x, w, res)
    return res


def _ple_embed_body(p_ref, w_ref, g_ref, o_ref):
    e = _dot(p_ref[...].astype(BF16), w_ref[...].astype(BF16))
    o_ref[...] = _rms(e, g_ref[...]).astype(o_ref.dtype)


def ple_embed(p, w_pe, layer, g):
    t, kp = p.shape
    d = w_pe.shape[-1]
    tm = _tile(t, 512, SUBLANE)
    vmem = 2 * (tm * kp * 4 + kp * d * 4 + tm * d * 2) + 2 * tm * d * 4 + kp * d * 2 + (4 << 20)
    return pl.pallas_call(
        _ple_embed_body, grid=(t // tm,),
        in_specs=[pl.BlockSpec((tm, kp), lambda i: (i, 0)),
                  pl.BlockSpec((None, kp, d), lambda i: (layer, 0, 0)),
                  pl.BlockSpec((1, d), lambda i: (0, 0))],
        out_specs=pl.BlockSpec((tm, d), lambda i: (i, 0)),
        out_shape=jax.ShapeDtypeStruct((t, d), BF16),
        compiler_params=_cparams(("parallel",), vmem), name="ple_embed")(p, w_pe, g.reshape(1, d))


def _rotate(xg, tab):
    return xg * tab[:, 0:LANE] + pltpu.roll(xg, ROPE // 2, 1) * tab[:, LANE:2 * LANE]


def _qkv_post_body(z_ref, tab_ref, wuq_ref, wuk_ref, gqa_ref, gq_ref, gkva_ref, gk_ref, *outs,
                   nh, qlr, sample, q_scale):
    if sample:
        q_ref, ckv_ref, kr_ref = outs
    else:
        q_ref, ckv_ref, kr_ref, kk_ref = outs
    tm = z_ref.shape[0]
    tab = tab_ref[...]
    lane = lax.broadcasted_iota(jnp.int32, (tm, LANE), 1)
    lo = lane < ROPE

    qa = _rms(z_ref[:, 0:qlr], gqa_ref[...]).astype(BF16)
    q = _dot(qa, wuq_ref[...])
    gq = gq_ref[...]
    for h in range(nh):
        qn = q[:, h * QH:h * QH + NOPE]
        qr = q[:, h * QH + NOPE:(h + 1) * QH]
        ssq = jnp.sum(qn * qn, axis=-1, keepdims=True) + jnp.sum(jnp.where(lo, qr * qr, 0.0), axis=-1, keepdims=True)
        sc = lax.rsqrt(ssq * (1.0 / (NOPE + ROPE)) + RMS_EPS)
        qn = (qn * sc * gq[:, 0:NOPE]).astype(BF16)
        qlat = _dot(qn, wuk_ref[h]) * q_scale
        rot = jnp.where(lo, _rotate(qr * sc * gq[:, NOPE:QH], tab), 0.0) * q_scale
        if sample:
            q_ref[:, h, :, 0:KVR] = qlat.reshape(tm // SUBLANE, SUBLANE, KVR)
            q_ref[:, h, :, KVR:QW] = rot.reshape(tm // SUBLANE, SUBLANE, LANE)
        else:
            q_ref[h, :, 0:KVR] = qlat.astype(q_ref.dtype)
            q_ref[h, :, KVR:QW] = rot.astype(q_ref.dtype)

    c = _rms(z_ref[:, qlr:qlr + KVR], gkva_ref[...])
    ckv_ref[...] = c
    k1 = z_ref[:, qlr + KVR:qlr + KVR + LANE]
    k2 = jnp.where(lo, k1, pltpu.roll(k1, ROPE, 1))
    ssq = jnp.sum(jnp.where(lo, k2 * k2, 0.0), axis=-1, keepdims=True)
    kr = _rotate(k2 * lax.rsqrt(ssq * (1.0 / ROPE) + RMS_EPS) * gk_ref[...], tab)
    kr_ref[...] = kr[:, 0:ROPE]
    if not sample:
        kk_ref[:, 0:KVR] = c.astype(BF16)
        kk_ref[:, KVR:QW] = jnp.where(lo, kr, 0.0).astype(BF16)


def qkv_post(z1, row0, rows, tab, tab_period, wuq, wuk_t, gqa, gq, gkva, gk, nh, sample, q_scale):
    qlr = gqa.shape[-1]
    tm = _tile(rows, 256, 16) if not sample else _tile(rows, 128, SUBLANE)
    assert row0 % tm == 0 and tab_period % tm == 0
    rb0 = row0 // tm
    tb = tab_period // tm
    zw = z1.shape[1]
    in_specs = [pl.BlockSpec((tm, zw), lambda i: (rb0 + i, 0)),
                pl.BlockSpec((tm, 2 * LANE), lambda i: (i % tb, 0)),
                pl.BlockSpec(wuq.shape, lambda i: (0, 0)),
                pl.BlockSpec(wuk_t.shape, lambda i: (0, 0, 0)),
                pl.BlockSpec((1, qlr), lambda i: (0, 0)),
                pl.BlockSpec((1, QH), lambda i: (0, 0)),
                pl.BlockSpec((1, KVR), lambda i: (0, 0)),
                pl.BlockSpec((1, LANE), lambda i: (0, 0))]
    if sample:
        q_shape = jax.ShapeDtypeStruct((rows // SUBLANE, nh, SUBLANE, QW), F32)
        q_spec = pl.BlockSpec((tm // SUBLANE, nh, SUBLANE, QW), lambda i: (i, 0, 0, 0))
    else:
        q_shape = jax.ShapeDtypeStruct((nh, rows, QW), BF16)
        q_spec = pl.BlockSpec((nh, tm, QW), lambda i: (0, i, 0))
    out_shape = [q_shape, jax.ShapeDtypeStruct((rows, KVR), F32), jax.ShapeDtypeStruct((rows, ROPE), F32)]
    out_specs = [q_spec, pl.BlockSpec((tm, KVR), lambda i: (i, 0)), pl.BlockSpec((tm, ROPE), lambda i: (i, 0))]
    if not sample:
        out_shape.append(jax.ShapeDtypeStruct((rows, QW), BF16))
        out_specs.append(pl.BlockSpec((tm, QW), lambda i: (i, 0)))
    vmem = 2 * (tm * zw * 4 + wuq.size * 2 + wuk_t.size * 2 + nh * tm * QW * 4) + 3 * tm * nh * QH * 4 + (8 << 20)
    return pl.pallas_call(
        functools.partial(_qkv_post_body, nh=nh, qlr=qlr, sample=sample, q_scale=q_scale),
        grid=(rows // tm,), in_specs=in_specs, out_specs=out_specs, out_shape=out_shape,
        compiler_params=_cparams(("parallel",), vmem), name="qkv_post_s" if sample else "qkv_post_p",
    )(z1, tab, wuq, wuk_t, gqa.reshape(1, qlr), gq, gkva.reshape(1, KVR), gk)


def _attn_p_body(q_ref, kk_ref, wuv_ref, gao_ref, o_ref, m_sc, l_sc, acc_sc, *, nh, tq, tk):
    qi = pl.program_id(1)
    ki = pl.program_id(2)
    last_k = (qi * tq + tq - 1) // tk
    rows = nh * tq

    @pl.when(ki == 0)
    def _():
        m_sc[...] = jnp.full(m_sc.shape, -jnp.inf, F32)
        l_sc[...] = jnp.zeros(l_sc.shape, F32)
        acc_sc[...] = jnp.zeros(acc_sc.shape, F32)

    def update(masked):
        q = q_ref[...].reshape(rows, QW)
        k = kk_ref[...]
        s = _dot_nt(q, k)
        if masked:
            qpos = (lax.broadcasted_iota(jnp.int32, (rows, tk), 0) & (tq - 1)) + qi * tq
            kpos = lax.broadcasted_iota(jnp.int32, (rows, tk), 1) + ki * tk
            s = jnp.where(kpos <= qpos, s, -jnp.inf)
        m_prev = m_sc[...]
        m_new = jnp.maximum(m_prev, jnp.max(s, axis=-1, keepdims=True))
        alpha = jnp.exp2(m_prev - m_new)
        p = jnp.exp2(s - m_new)
        l_sc[...] = alpha * l_sc[...] + jnp.sum(p, axis=-1, keepdims=True)
        acc_sc[...] = alpha * acc_sc[...] + _dot(p.astype(BF16), k[:, 0:KVR])
        m_sc[...] = m_new

    @pl.when(ki < last_k)
    def _():
        update(False)

    @pl.when(ki == last_k)
    def _():
        update(True)
        inv = 1.0 / l_sc[...]
        for h in range(nh):
            sl = slice(h * tq, (h + 1) * tq)
            o = (acc_sc[sl, :] * inv[sl, :]).astype(BF16)
            v = _dot(o, wuv_ref[h])
            o_ref[:, h * VH:(h + 1) * VH] = _rms(v, gao_ref[h]).astype(o_ref.dtype)


def attn_prompt(q, kk, wuv_t, g_ao, b, s, nh):
    tq = _tile(s, 128, 16)
    tk = _tile(s, 512, 16)
    assert tq & (tq - 1) == 0 and tk % tq == 0
    nq, nk = s // tq, s // tk
    rows = nh * tq

    def kk_map(bi, qi, ki):
        return (bi * nk + jnp.minimum(ki, (qi * tq + tq - 1) // tk), 0)

    vmem = 2 * (rows * QW * 2 + tk * QW * 2 + tq * nh * VH * 2 + wuv_t.size * 2) + rows * (KVR + 2 * LANE) * 4 \
        + 4 * rows * tk * 4 + (8 << 20)
    return pl.pallas_call(
        functools.partial(_attn_p_body, nh=nh, tq=tq, tk=tk),
        grid=(b, nq, nk),
        in_specs=[pl.BlockSpec((nh, tq, QW), lambda bi, qi, ki: (0, bi * nq + qi, 0)),
                  pl.BlockSpec((tk, QW), kk_map),
                  pl.BlockSpec(wuv_t.shape, lambda bi, qi, ki: (0, 0, 0)),
                  pl.BlockSpec(g_ao.shape, lambda bi, qi, ki: (0, 0, 0))],
        out_specs=pl.BlockSpec((tq, nh * VH), lambda bi, qi, ki: (bi * nq + qi, 0)),
        out_shape=jax.ShapeDtypeStruct((b * s, nh * VH), BF16),
        scratch_shapes=[pltpu.VMEM((rows, 1), F32), pltpu.VMEM((rows, 1), F32), pltpu.VMEM((rows, KVR), F32)],
        compiler_params=_cparams(("parallel", "parallel", "arbitrary"), vmem), name="attn_prompt",
    )(q, kk, wuv_t, g_ao)


def _attn_s_body(pt_ref, q_ref, cn_ref, kn_ref, *rest, nh, ds, gp, page):
    c_refs = rest[0:gp]
    k_refs = rest[gp:2 * gp]
    o_ref, m_sc, l_sc, acc_sc = rest[2 * gp:]
    g = pl.program_id(1)
    rows = nh * ds
    q = q_ref[...].reshape(rows, QW).astype(BF16)
    q_lat = q[:, 0:KVR]
    q_rope = q[:, KVR:KVR + ROPE]

    def update(s, v):
        m_prev = m_sc[...]
        m_new = jnp.maximum(m_prev, jnp.max(s, axis=-1, keepdims=True))
        alpha = jnp.exp2(m_prev - m_new)
        p = jnp.exp2(s - m_new)
        l_sc[...] = alpha * l_sc[...] + jnp.sum(p, axis=-1, keepdims=True)
        acc_sc[...] = alpha * acc_sc[...] + _dot(p.astype(BF16), v)
        m_sc[...] = m_new

    @pl.when(g == 0)
    def _():
        m_sc[...] = jnp.full(m_sc.shape, -jnp.inf, F32)
        l_sc[...] = jnp.zeros(l_sc.shape, F32)
        acc_sc[...] = jnp.zeros(acc_sc.shape, F32)
        pad = LANE - ds
        c0 = jnp.concatenate([cn_ref[...], jnp.zeros((pad, KVR), F32)], axis=0).astype(BF16)
        k0 = jnp.concatenate([kn_ref[...], jnp.zeros((pad, ROPE), F32)], axis=0).astype(BF16)
        s = _dot_nt(q_lat, c0) + _dot_nt(q_rope, k0)
        tpos = lax.broadcasted_iota(jnp.int32, (rows, LANE), 0) & (ds - 1)
        kpos = lax.broadcasted_iota(jnp.int32, (rows, LANE), 1)
        update(jnp.where(kpos <= tpos, s, -jnp.inf), c0)

    c = jnp.concatenate([r[...] for r in c_refs], axis=0).astype(BF16)
    kt = jnp.concatenate([r[...] for r in k_refs], axis=1).astype(BF16)
    update(_dot_nt(q_lat, c) + _dot(q_rope, kt), c)

    @pl.when(g == pl.num_programs(1) - 1)
    def _():
        o = acc_sc[...] / l_sc[...]
        o_ref[...] = o.reshape(nh, ds, KVR)


def attn_sample(q, c_new, k_new, cache_ckv, cache_krope_t, page_table, layer, nh):
    db, _, ds, _ = q.shape
    n_pages = page_table.shape[1]
    page = cache_ckv.shape[2]
    assert ds == SUBLANE and ds & (ds - 1) == 0
    gp = next(g for g in (32, 16, 8, 4, 2, 1) if n_pages % g == 0)
    ng = n_pages // gp
    rows = nh * ds
    pt = page_table.reshape(-1)

    def page_spec(shape, i):
        return pl.BlockSpec((None, None) + shape,
                            lambda b, g, pt_ref: (layer, pt_ref[b * n_pages + g * gp + i], 0, 0))

    in_specs = [pl.BlockSpec((None, nh, ds, QW), lambda b, g, pt_ref: (b, 0, 0, 0)),
                pl.BlockSpec((ds, KVR), lambda b, g, pt_ref: (b, 0)),
                pl.BlockSpec((ds, ROPE), lambda b, g, pt_ref: (b, 0))]
    in_specs += [page_spec((page, KVR), i) for i in range(gp)] + [page_spec((ROPE, page), i) for i in range(gp)]
    grid_spec = pltpu.PrefetchScalarGridSpec(
        num_scalar_prefetch=1, grid=(db, ng), in_specs=in_specs,
        out_specs=pl.BlockSpec((None, nh, ds, KVR), lambda b, g, pt_ref: (b, 0, 0, 0)),
        scratch_shapes=[pltpu.VMEM((rows, 1), F32), pltpu.VMEM((rows, 1), F32), pltpu.VMEM((rows, KVR), F32)])
    keys = gp * page
    vmem = 2 * keys * (KVR + ROPE) * 4 + 3 * keys * (KVR + ROPE) * 2 + 4 * rows * keys * 4 + (8 << 20)
    return pl.pallas_call(
        functools.partial(_attn_s_body, nh=nh, ds=ds, gp=gp, page=page),
        grid_spec=grid_spec, out_shape=jax.ShapeDtypeStruct((db, nh, ds, KVR), F32),
        compiler_params=_cparams(("parallel", "arbitrary"), vmem), name="attn_sample",
    )(pt, q, c_new, k_new, *([cache_ckv] * gp), *([cache_krope_t] * gp))


def _ouv_body(o_ref, w_ref, g_ref, out_ref):
    bx, by, _ = o_ref.shape
    o = o_ref[...].reshape(bx * by, KVR).astype(BF16)
    v = _dot(o, w_ref[...])
    out_ref[...] = _rms(v, g_ref[...]).astype(out_ref.dtype)


def ouv(o_lat, wuv_t, g_ao, bx, by):
    x, nh, y, _ = o_lat.shape
    tm = bx * by
    nyb = y // by
    vmem = 2 * (tm * KVR * 4 + KVR * VH * 2 + tm * VH * 2) + 4 * tm * KVR * 4 + (4 << 20)
    return pl.pallas_call(
        _ouv_body, grid=(x // bx, nyb, nh),
        in_specs=[pl.BlockSpec((bx, None, by, KVR), lambda xi, yi, h: (xi, h, yi, 0)),
                  pl.BlockSpec((None, KVR, VH), lambda xi, yi, h: (h, 0, 0)),
                  pl.BlockSpec((None, 1, VH), lambda xi, yi, h: (h, 0, 0))],
        out_specs=pl.BlockSpec((tm, VH), lambda xi, yi, h: (xi * nyb + yi, h)),
        out_shape=jax.ShapeDtypeStruct((x * y, nh * VH), BF16),
        compiler_params=_cparams(("parallel", "parallel", "arbitrary"), vmem), name="ouv")(o_lat, wuv_t, g_ao)


def _conv_core(gb, gc, xc, p0, p1, period, w_ref, g_ref, y_ref, groups):
    tm, c = gb.shape
    u = gc.astype(F32) * xc.astype(F32)
    t = lax.broadcasted_iota(jnp.int32, (tm, c), 0) & (period - 1)
    um1 = jnp.where(t == 0, p1, pltpu.roll(u, 1, 0))
    um2 = jnp.where(t == 0, p0, jnp.where(t == 1, p1, pltpu.roll(u, 2, 0)))
    w = w_ref[...]
    y = um2 * w[0:1, :] + um1 * w[1:2, :] + u * w[2:3, :]
    y = gb.astype(F32) * y
    gw = c // groups
    g = g_ref[...]
    for i in range(groups):
        sl = slice(i * gw, (i + 1) * gw)
        y_ref[:, sl] = _rms(y[:, sl], g[:, sl]).astype(y_ref.dtype)
    return u


def _conv_p_body(gb_ref, gc_ref, xc_ref, w_ref, g_ref, y_ref, st_ref, prev_sc, *, groups):
    ts = gb_ref.shape[0]

    @pl.when(pl.program_id(1) == 0)
    def _():
        prev_sc[...] = jnp.zeros(prev_sc.shape, F32)

    u = _conv_core(gb_ref[...], gc_ref[...], xc_ref[...], prev_sc[0:1, :], prev_sc[1:2, :], ts,
                   w_ref, g_ref, y_ref, groups)
    tail = u[ts - 2:ts, :]
    prev_sc[0:2, :] = tail
    st_ref[...] = tail


def conv_prompt(z2, b, s, conv_w, g_co, groups):
    c = conv_w.shape[-1]
    ts = _tile(s, 256, 16)
    assert ts & (ts - 1) == 0 and conv_w.shape[0] == 3
    ns = s // ts

    def col(j):
        return pl.BlockSpec((ts, c), lambda bi, si: (bi * ns + si, j))

    vmem = 2 * (3 * ts * c * 2 + ts * c * 2) + 8 * ts * c * 4 + (4 << 20)
    return pl.pallas_call(
        functools.partial(_conv_p_body, groups=groups), grid=(b, ns),
        in_specs=[col(0), col(1), col(2), pl.BlockSpec((3, c), lambda bi, si: (0, 0)),
                  pl.BlockSpec((1, c), lambda bi, si: (0, 0))],
        out_specs=[pl.BlockSpec((ts, c), lambda bi, si: (bi * ns + si, 0)),
                   pl.BlockSpec((None, 2, c), lambda bi, si: (bi, 0, 0))],
        out_shape=[jax.ShapeDtypeStruct((b * s, c), BF16), jax.ShapeDtypeStruct((b, 2, c), F32)],
        scratch_shapes=[pltpu.VMEM((SUBLANE, c), F32)],
        compiler_params=_cparams(("parallel", "arbitrary"), vmem), name="conv_prompt",
    )(z2, z2, z2, conv_w, g_co.reshape(1, c))


def _conv_s_body(gb_ref, gc_ref, xc_ref, st_ref, w_ref, g_ref, y_ref, new_ref, *, groups, ds):
    tm, c = gb_ref.shape
    bb = tm // ds
    st = st_ref[...]
    p0 = jnp.broadcast_to(st[:, 0:1, :], (bb, ds, c)).reshape(tm, c)
    p1 = jnp.broadcast_to(st[:, 1:2, :], (bb, ds, c)).reshape(tm, c)
    u = _conv_core(gb_ref[...], gc_ref[...], xc_ref[...], p0, p1, ds, w_ref, g_ref, y_ref, groups)
    new_ref[...] = u.reshape(bb, ds, c)[:, ds - 2:ds, :]


def conv_sample(z2, row0, db, ds, state, conv_w, g_co, groups):
    c = conv_w.shape[-1]
    assert ds == SUBLANE
    bb = _tile(db, 16, 1)
    tm = bb * ds
    assert row0 % tm == 0
    rb0 = row0 // tm

    def col(j):
        return pl.BlockSpec((tm, c), lambda i: (rb0 + i, j))

    vmem = 2 * (3 * tm * c * 2 + tm * c * 2 + 2 * bb * SUBLANE * c * 4) + 10 * tm * c * 4 + (4 << 20)
    return pl.pallas_call(
        functools.partial(_conv_s_body, groups=groups, ds=ds), grid=(db // bb,),
        in_specs=[col(0), col(1), col(2), pl.BlockSpec((bb, 2, c), lambda i: (i, 0, 0)),
                  pl.BlockSpec((3, c), lambda i: (0, 0)), pl.BlockSpec((1, c), lambda i: (0, 0))],
        out_specs=[pl.BlockSpec((tm, c), lambda i: (i, 0)), pl.BlockSpec((bb, 2, c), lambda i: (i, 0, 0))],
        out_shape=[jax.ShapeDtypeStruct((db * ds, c), BF16), jax.ShapeDtypeStruct((db, 2, c), F32)],
        compiler_params=_cparams(("parallel",), vmem), name="conv_sample",
    )(z2, z2, z2, state, conv_w, g_co.reshape(1, c))


def _gather_body(idx_ref, valid_ref, src_ref, o_ref, buf, sem, *, rows):
    i = pl.program_id(0)

    def row_copy(r, src_row):
        return pltpu.make_async_copy(src_ref.at[pl.ds(src_row, 1)], buf.at[pl.ds(r, 1)], sem)

    @pl.when(valid_ref[i] != 0)
    def _():
        def issue(r, carry):
            row_copy(r, idx_ref[i * rows + r]).start()
            return carry

        lax.fori_loop(0, rows, issue, 0)

        def wait(r, carry):
            row_copy(r, 0).wait()
            return carry

        lax.fori_loop(0, rows, wait, 0)
        o_ref[...] = buf[...].astype(o_ref.dtype)

    @pl.when(valid_ref[i] == 0)
    def _():
        o_ref[...] = jnp.zeros(o_ref.shape, o_ref.dtype)


def gather_rows(src, idx, tile_valid, rows):
    p = idx.shape[0]
    d = src.shape[1]
    grid_spec = pltpu.PrefetchScalarGridSpec(
        num_scalar_prefetch=2, grid=(p // rows,),
        in_specs=[pl.BlockSpec(memory_space=pl.ANY)],
        out_specs=pl.BlockSpec((rows, d), lambda i, idx_ref, v_ref: (i, 0)),
        scratch_shapes=[pltpu.VMEM((rows, d), src.dtype), pltpu.SemaphoreType.DMA(())])
    vmem = rows * d * 4 + 2 * rows * d * 2 + rows * d * 4 + (4 << 20)
    return pl.pallas_call(
        functools.partial(_gather_body, rows=rows), grid_spec=grid_spec,
        out_shape=jax.ShapeDtypeStruct((p, d), BF16),
        compiler_params=_cparams(("arbitrary",), vmem), name="moe_gather")(idx, tile_valid, src)


def _combine_body(p0_ref, p1_ref, ys_ref, h_ref, o_ref, b0, b1, sem, *, rows):
    i = pl.program_id(0)

    def row_copy(r, src_row, buf):
        return pltpu.make_async_copy(ys_ref.at[pl.ds(src_row, 1)], buf.at[pl.ds(r, 1)], sem)

    def issue(r, carry):
        row_copy(r, p0_ref[i * rows + r], b0).start()
        row_copy(r, p1_ref[i * rows + r], b1).start()
        return carry

    lax.fori_loop(0, rows, issue, 0)

    def wait(r, carry):
        row_copy(r, 0, b0).wait()
        row_copy(r, 0, b1).wait()
        return carry

    lax.fori_loop(0, rows, wait, 0)
    o_ref[...] = h_ref[...] + b0[...] + b1[...]


def moe_combine(ys, pos0, pos1, h, rows=256):
    t, d = h.shape
    rows = _tile(t, rows, SUBLANE)
    grid_spec = pltpu.PrefetchScalarGridSpec(
        num_scalar_prefetch=2, grid=(t // rows,),
        in_specs=[pl.BlockSpec(memory_space=pl.ANY), pl.BlockSpec((rows, d), lambda i, a, b: (i, 0))],
        out_specs=pl.BlockSpec((rows, d), lambda i, a, b: (i, 0)),
        scratch_shapes=[pltpu.VMEM((rows, d), F32), pltpu.VMEM((rows, d), F32), pltpu.SemaphoreType.DMA(())])
    vmem = 6 * rows * d * 4 + rows * d * 4 + (4 << 20)
    return pl.pallas_call(
        functools.partial(_combine_body, rows=rows), grid_spec=grid_spec,
        out_shape=jax.ShapeDtypeStruct((t, d), F32),
        compiler_params=_cparams(("arbitrary",), vmem), name="moe_combine")(pos0, pos1, ys, h)


def _new_expert(te_ref):
    i = pl.program_id(1)
    return jnp.logical_or(i == 0, te_ref[i] != te_ref[jnp.maximum(i - 1, 0)])


def _moe_up_body(te_ref, tb_ref, tv_ref, x_ref, wg_ref, wu_ref, o_ref, wg_sc, wu_sc):
    valid = tv_ref[pl.program_id(1)] != 0

    @pl.when(jnp.logical_and(valid, _new_expert(te_ref)))
    def _():
        wg_sc[...] = wg_ref[...].astype(BF16)
        wu_sc[...] = wu_ref[...].astype(BF16)

    @pl.when(valid)
    def _():
        x = x_ref[...]
        g = _dot(x, wg_sc[...])
        u = _dot(x, wu_sc[...])
        o_ref[...] = (g * _sigmoid(g) * u).astype(o_ref.dtype)

    @pl.when(jnp.logical_not(valid))
    def _():
        o_ref[...] = jnp.zeros(o_ref.shape, o_ref.dtype)


def moe_up_call(xs, wg, wu, j, tile_expert, tile_block, tile_valid, tm, tf_pref=256):
    p, k = xs.shape
    f = wg.shape[-1]
    tf = _tile(f, tf_pref)

    def w_map(fi, i, te, tb, tv):
        return (j, te[i], 0, fi)

    grid_spec = pltpu.PrefetchScalarGridSpec(
        num_scalar_prefetch=3, grid=(f // tf, p // tm),
        in_specs=[pl.BlockSpec((tm, k), lambda fi, i, te, tb, tv: (tb[i], 0)),
                  pl.BlockSpec((None, None, k, tf), w_map), pl.BlockSpec((None, None, k, tf), w_map)],
        out_specs=pl.BlockSpec((tm, tf), lambda fi, i, te, tb, tv: (i, fi)),
        scratch_shapes=[pltpu.VMEM((k, tf), BF16), pltpu.VMEM((k, tf), BF16)])
    vmem = 2 * (tm * k * 2 + 2 * k * tf * 4 + tm * tf * 2) + 2 * k * tf * 2 + 3 * tm * tf * 4 + (4 << 20)
    return pl.pallas_call(
        _moe_up_body, grid_spec=grid_spec, out_shape=jax.ShapeDtypeStruct((p, f), BF16),
        compiler_params=_cparams(("parallel", "arbitrary"), vmem), name="moe_up",
    )(tile_expert, tile_block, tile_valid, xs, wg, wu)


def _moe_down_body(te_ref, tb_ref, tv_ref, a_ref, w_ref, g_ref, o_ref, w_sc):
    valid = tv_ref[pl.program_id(1)] != 0

    @pl.when(jnp.logical_and(valid, _new_expert(te_ref)))
    def _():
        w_sc[...] = w_ref[...].astype(BF16)

    @pl.when(valid)
    def _():
        o_ref[...] = _dot(a_ref[...], w_sc[...]) * g_ref[...]

    @pl.when(jnp.logical_not(valid))
    def _():
        o_ref[...] = jnp.zeros(o_ref.shape, o_ref.dtype)


def moe_down_call(act, wd, j, row_gate, tile_expert, tile_block, tile_valid, tm, tn_pref=512):
    p, k = act.shape
    n = wd.shape[-1]
    tn = _tile(n, tn_pref)
    grid_spec = pltpu.PrefetchScalarGridSpec(
        num_scalar_prefetch=3, grid=(n // tn, p // tm),
        in_specs=[pl.BlockSpec((tm, k), lambda ni, i, te, tb, tv: (tb[i], 0)),
                  pl.BlockSpec((None, None, k, tn), lambda ni, i, te, tb, tv: (j, te[i], 0, ni)),
                  pl.BlockSpec((tm, 1), lambda ni, i, te, tb, tv: (tb[i], 0))],
        out_specs=pl.BlockSpec((tm, tn), lambda ni, i, te, tb, tv: (i, ni)),
        scratch_shapes=[pltpu.VMEM((k, tn), BF16)])
    vmem = 2 * (tm * k * 2 + k * tn * 4 + tm * tn * 4 + tm * LANE * 4) + k * tn * 2 + tm * tn * 4 + (4 << 20)
    return pl.pallas_call(
        _moe_down_body, grid_spec=grid_spec, out_shape=jax.ShapeDtypeStruct((p, n), F32),
        compiler_params=_cparams(("parallel", "arbitrary"), vmem), name="moe_down",
    )(tile_expert, tile_block, tile_valid, act, wd, row_gate)


def moe_route(logits, n_exp, tm):
    t = logits.shape[0]
    top_val, top_idx = lax.top_k(logits[:, :n_exp], TOP_K)
    wts = jax.nn.softmax(top_val, axis=-1)
    e_flat = top_idx.T.reshape(-1).astype(jnp.int32)
    g_flat = wts.T.reshape(-1)
    tok = jnp.tile(jnp.arange(t, dtype=jnp.int32), TOP_K)
    counts = jnp.sum(jax.nn.one_hot(e_flat, n_exp, dtype=jnp.int32), axis=0)
    padded = ((counts + tm - 1) // tm) * tm
    pad_end = jnp.cumsum(padded)
    pad_start = pad_end - padded
    start = jnp.cumsum(counts) - counts
    order = jnp.argsort(e_flat, stable=True)
    e_sorted = e_flat[order]
    dest_sorted = pad_start[e_sorted] + jnp.arange(TOP_K * t, dtype=jnp.int32) - start[e_sorted]
    p = TOP_K * t + n_exp * tm
    row_token = jnp.zeros((p,), jnp.int32).at[dest_sorted].set(tok[order])
    row_gate = jnp.zeros((p,), F32).at[dest_sorted].set(g_flat[order])
    dest = jnp.zeros((TOP_K * t,), jnp.int32).at[order].set(dest_sorted)
    nt = p // tm
    tile_start = jnp.arange(nt, dtype=jnp.int32) * tm
    total = pad_end[-1]
    tile_valid = (tile_start < total).astype(jnp.int32)
    last_tile = total // tm - 1
    tile_block = jnp.minimum(jnp.arange(nt, dtype=jnp.int32), last_tile).astype(jnp.int32)
    tile_expert = jnp.minimum(jnp.searchsorted(pad_end, tile_block * tm, side="right"), n_exp - 1).astype(jnp.int32)
    return row_token, row_gate.reshape(p, 1), dest[:t], dest[t:], tile_expert, tile_block, tile_valid


def _rope_table(pos):
    half = ROPE // 2
    inv_freq = jnp.exp(-math.log(ROPE_THETA) * jnp.arange(half, dtype=F32) / half)
    ang = pos.astype(F32)[:, None] * inv_freq[None, :]
    cos, sin = jnp.cos(ang), jnp.sin(ang)
    return jnp.concatenate([cos, cos, cos, cos, -sin, sin, -sin, sin], axis=-1)


def kernel(x_prompt, x_sample, p_prompt, p_sample, cache_ckv, cache_krope, state_conv, page_table, g_mix, w_in, g_qa, w_uq, g_qn, g_kva, g_kn, w_uk, w_uv, conv_w, g_ao, g_co, w_o, g_ffn, w_ffn_gate, w_ffn_up, w_ffn_down, w_router, moe_gate, moe_up, moe_down, w_pe, g_pe, g_pg, w_pg):
    b, s, d = x_prompt.shape
    db, ds, _ = x_sample.shape
    depth = w_in.shape[0]
    qlr = g_qa.shape[-1]
    nh = w_uq.shape[2]
    c_dim = conv_w.shape[-1]
    groups = g_co.shape[1]
    n_exp = w_router.shape[-1]
    assert w_uq.shape[3] == NOPE + ROPE and g_kva.shape[-1] == KVR and g_kn.shape[-1] == ROPE
    assert w_uv.shape[-1] == VH and cache_ckv.shape[-1] == KVR and cache_krope.shape[-1] == ROPE
    assert c_dim // groups == LANE and nh * VH == c_dim
    tp, tsmp = b * s, db * ds
    past_len = page_table.shape[1] * cache_ckv.shape[2]
    q_scale = float((NOPE + ROPE) ** -0.5) * LOG2E
    c0 = qlr + KVR + ROPE
    z1w = qlr + KVR + LANE
    moe_tm = 512

    h = jnp.concatenate([x_prompt.reshape(tp, d), x_sample.reshape(tsmp, d)], axis=0)
    p_all = jnp.concatenate([p_prompt.reshape(depth, tp, -1), p_sample.reshape(depth, tsmp, -1)], axis=1)
    tab_p = _rope_table(jnp.arange(s))
    tab_s = jnp.tile(_rope_table(past_len + jnp.arange(ds)), (LANE // ds, 1))
    w_in_t = jnp.swapaxes(w_in, 1, 2)
    cache_krope_t = jnp.swapaxes(cache_krope, 2, 3)

    outs = [[] for _ in range(6)]
    for i in range(depth):
        wq = w_uq[i]
        wuq = jnp.concatenate([wq, wq[..., NOPE:]], axis=-1).reshape(qlr, nh * QH).astype(BF16)
        wuk_t = jnp.transpose(w_uk[i], (1, 2, 0)).astype(BF16)
        wuv_t = jnp.transpose(w_uv[i], (1, 0, 2)).astype(BF16)
        gq = jnp.concatenate([g_qn[i], g_qn[i, NOPE:]]).reshape(1, QH)
        gk = jnp.concatenate([g_kn[i], g_kn[i]]).reshape(1, LANE)
        g_ao_i = g_ao[i].reshape(nh, 1, VH)

        (a,) = rmsnorm(h, g_mix[i])
        z1 = mm_nt(a, w_in_t, i, 0, z1w, F32, tn_pref=384)
        z2 = mm_nt(a, w_in_t, i, c0, 3 * c_dim, BF16)
        q_p, ckv_p, kr_p, kk_p = qkv_post(z1, 0, tp, tab_p, s, wuq, wuk_t, g_qa[i], gq, g_kva[i], gk,
                                          nh, False, q_scale)
        q_s, ckv_s, kr_s = qkv_post(z1, tp, tsmp, tab_s, tab_s.shape[0], wuq, wuk_t, g_qa[i], gq, g_kva[i], gk,
                                    nh, True, q_scale)
        o_p = attn_prompt(q_p, kk_p, wuv_t, g_ao_i, b, s, nh)
        ol_s = attn_sample(q_s, ckv_s, kr_s, cache_ckv, cache_krope_t, page_table, i, nh)
        o_s = ouv(ol_s, wuv_t, g_ao_i, _tile(db, 64, 1), ds)
        yc_p, cv_p = conv_prompt(z2, b, s, conv_w[i], g_co[i], groups)
        yc_s, cv_s = conv_sample(z2, tp, db, ds, state_conv[i], conv_w[i], g_co[i], groups)
        o_all = jnp.concatenate([o_p, o_s], axis=0)
        yc_all = jnp.concatenate([yc_p, yc_s], axis=0)
        h = mm2_res(o_all, yc_all, w_o, (i,), h)

        j = i // 2
        if i % 2 == 0:
            (f,) = rmsnorm(h, g_ffn[i])
            act = mm_swiglu(f, w_ffn_gate, w_ffn_up, (j,))
            h = mm_down(act, w_ffn_down, (j,), h)
        else:
            f, logits = rmsnorm(h, g_ffn[i], router_w=w_router[j], y_dtype=F32)
            row_token, row_gate, pos0, pos1, t_exp, t_blk, t_val = moe_route(logits, n_exp, moe_tm)
            xs = gather_rows(f, row_token, t_val, moe_tm)
            act = moe_up_call(xs, moe_gate, moe_up, j, t_exp, t_blk, t_val, moe_tm)
            ys = moe_down_call(act, moe_down, j, row_gate, t_exp, t_blk, t_val, moe_tm)
            h = moe_combine(ys, pos0, pos1, h)

        (hn,) = rmsnorm(h, g_pg[i])
        e = ple_embed(p_all[i], w_pe, i, g_pe[i])
        h = mm_ple(hn, w_pg, (i,), h, e)

        for lst, v in zip(outs, (ckv_p, kr_p, cv_p, ckv_s, kr_s, cv_s)):
            lst.append(v)

    ckv_prompt = jnp.stack(outs[0]).reshape(depth, b, s, KVR)
    krope_prompt = jnp.stack(outs[1]).reshape(depth, b, s, ROPE)
    conv_prompt_out = jnp.stack(outs[2])
    ckv_sample = jnp.stack(outs[3]).reshape(depth, db, ds, KVR)
    krope_sample = jnp.stack(outs[4]).reshape(depth, db, ds, ROPE)
    conv_sample_out = jnp.stack(outs[5])
    return (h[:tp].reshape(b, s, d), h[tp:].reshape(db, ds, d), ckv_prompt, krope_prompt, conv_prompt_out,
            ckv_sample, krope_sample, conv_sample_out)
```

```python
import functools
import math

import jax
import jax.numpy as jnp
from jax import lax
from jax.experimental import pallas as pl
from jax.experimental.pallas import tpu as pltpu

F32 = jnp.float32
BF16 = jnp.bfloat16
RMS_EPS = 1e-6
ROPE_THETA = 10000.0
TOP_K = 2
LOG2E = math.log2(math.e)

V7X_VMEM_BYTES = 64 * 1024 * 1024
VMEM_BUDGET = V7X_VMEM_BYTES - 8 * 1024 * 1024
LANE = 128
SUBLANE = 8

NOPE = 128
ROPE = 64
QH = NOPE + 2 * ROPE
KVR = 256
QW = KVR + LANE
VH = 128


def _cparams(sem, vmem_bytes):
    return pltpu.CompilerParams(dimension_semantics=sem,
                                vmem_limit_bytes=int(min(max(vmem_bytes, 16 << 20), VMEM_BUDGET)))


def _tile(n, pref, align=LANE):
    if n <= pref:
        return n
    t = (pref // align) * align
    while t >= align:
        if n % t == 0:
            return t
        t -= align
    return n


def _dot(a, b):
    return jnp.dot(a, b, preferred_element_type=F32)


def _dot_nt(a, b):
    return lax.dot_general(a, b, (((1,), (1,)), ((), ())), preferred_element_type=F32)


def _dot_tn(a, b):
    return lax.dot_general(a, b, (((0,), (0,)), ((), ())), preferred_element_type=F32)


def _rms(x, g):
    ms = jnp.mean(x * x, axis=-1, keepdims=True)
    return x * lax.rsqrt(ms + RMS_EPS) * g


def _sigmoid(x):
    return 1.0 / (1.0 + jnp.exp(-x))


def _first_inner_step():
    return pl.program_id(1) == 0


def _norm_body(*refs, n_add, router, y_dtype):
    x_ref = refs[0]
    add_refs = refs[1:1 + n_add]
    g_ref = refs[1 + n_add]
    pos = 2 + n_add
    wr_ref = refs[pos] if router else None
    pos += 1 if router else 0
    outs = refs[pos:]
    x = x_ref[...]
    for a in add_refs:
        x = x + a[...]
    oi = 0
    if n_add:
        outs[oi][...] = x
        oi += 1
    y = _rms(x, g_ref[...])
    outs[oi][...] = y.astype(y_dtype)
    oi += 1
    if router:
        outs[oi][...] = jnp.dot(y, wr_ref[...], preferred_element_type=F32,
                                precision=lax.Precision.HIGHEST)


def rmsnorm(x, g, adds=(), router_w=None, y_dtype=BF16):
    t, d = x.shape
    tm = _tile(t, 256, SUBLANE)
    n_add = len(adds)
    row = pl.BlockSpec((tm, d), lambda i: (i, 0))
    in_specs = [row] * (1 + n_add) + [pl.BlockSpec((1, d), lambda i: (0, 0))]
    args = [x, *adds, g.reshape(1, d)]
    out_shape, out_specs = [], []
    if n_add:
        out_shape.append(jax.ShapeDtypeStruct((t, d), F32))
        out_specs.append(row)
    out_shape.append(jax.ShapeDtypeStruct((t, d), y_dtype))
    out_specs.append(row)
    if router_w is not None:
        ne = router_w.shape[1]
        wr = jnp.zeros((d, LANE), F32).at[:, :ne].set(router_w)
        in_specs.append(pl.BlockSpec((d, LANE), lambda i: (0, 0)))
        args.append(wr)
        out_shape.append(jax.ShapeDtypeStruct((t, LANE), F32))
        out_specs.append(pl.BlockSpec((tm, LANE), lambda i: (i, 0)))
    vmem = 2 * tm * d * 4 * (2 + 2 * n_add) + (4 << 20)
    outs = pl.pallas_call(
        functools.partial(_norm_body, n_add=n_add, router=router_w is not None, y_dtype=y_dtype),
        grid=(t // tm,), in_specs=in_specs, out_specs=out_specs, out_shape=out_shape,
        compiler_params=_cparams(("parallel",), vmem), name="rmsnorm")(*args)
    return outs


def _w_spec(w, layer_idx, k, tn, kblock=0):
    lead = tuple(layer_idx)
    nlead = len(lead)
    assert w.ndim == nlead + 2
    return pl.BlockSpec((None,) * nlead + (k, tn), lambda j, i: lead + (kblock, j))


def _mm_nt_body(x_ref, w_ref, o_ref, wb_sc):
    @pl.when(_first_inner_step())
    def _():
        wb_sc[...] = w_ref[0].astype(BF16)

    o_ref[...] = _dot_nt(x_ref[...], wb_sc[...]).astype(o_ref.dtype)


def mm_nt(x, wt, layer, row0, n, out_dtype, tm_pref=1024, tn_pref=512):
    m, k = x.shape
    tm, tn = _tile(m, tm_pref, SUBLANE), _tile(n, tn_pref)
    assert row0 % SUBLANE == 0
    w_spec = pl.BlockSpec((pl.Element(1), pl.Element(tn), pl.Element(k)), lambda j, i: (layer, pl.multiple_of(row0 + j * tn, SUBLANE), 0))
    vmem = 2 * (tm * k * 2 + k * tn * 4 + tm * tn * 4) + k * tn * 2 + tm * tn * 4 + (4 << 20)
    return pl.pallas_call(
        _mm_nt_body, grid=(n // tn, m // tm),
        in_specs=[pl.BlockSpec((tm, k), lambda j, i: (i, 0)), w_spec],
        out_specs=pl.BlockSpec((tm, tn), lambda j, i: (i, j)),
        out_shape=jax.ShapeDtypeStruct((m, n), out_dtype),
        scratch_shapes=[pltpu.VMEM((tn, k), BF16)],
        compiler_params=_cparams(("parallel", "arbitrary"), vmem), name="mm_in")(x, wt)


def _mm_swiglu_body(x_ref, wg_ref, wu_ref, o_ref, wg_sc, wu_sc):
    @pl.when(_first_inner_step())
    def _():
        wg_sc[...] = wg_ref[...].astype(BF16)
        wu_sc[...] = wu_ref[...].astype(BF16)

    x = x_ref[...]
    g = _dot(x, wg_sc[...])
    u = _dot(x, wu_sc[...])
    o_ref[...] = (g * _sigmoid(g) * u).astype(o_ref.dtype)


def mm_swiglu(x, wg, wu, layer_idx, tm_pref=1024, tn_pref=256):
    m, k = x.shape
    n = wg.shape[-1]
    tm, tn = _tile(m, tm_pref, SUBLANE), _tile(n, tn_pref)
    vmem = 2 * (tm * k * 2 + 2 * k * tn * 4 + tm * tn * 2) + 2 * k * tn * 2 + 3 * tm * tn * 4 + (4 << 20)
    return pl.pallas_call(
        _mm_swiglu_body, grid=(n // tn, m // tm),
        in_specs=[pl.BlockSpec((tm, k), lambda j, i: (i, 0)),
                  _w_spec(wg, layer_idx, k, tn), _w_spec(wu, layer_idx, k, tn)],
        out_specs=pl.BlockSpec((tm, tn), lambda j, i: (i, j)),
        out_shape=jax.ShapeDtypeStruct((m, n), BF16),
        scratch_shapes=[pltpu.VMEM((k, tn), BF16), pltpu.VMEM((k, tn), BF16)],
        compiler_params=_cparams(("parallel", "arbitrary"), vmem), name="mm_swiglu")(x, wg, wu)


def _mm2_res_body(x1_ref, x2_ref, w1_ref, w2_ref, r_ref, o_ref, w1_sc, w2_sc):
    @pl.when(_first_inner_step())
    def _():
        w1_sc[...] = w1_ref[...].astype(BF16)
        w2_sc[...] = w2_ref[...].astype(BF16)

    acc = _dot(x1_ref[...], w1_sc[...]) + _dot(x2_ref[...], w2_sc[...])
    o_ref[...] = r_ref[...] + acc


def mm2_res(x1, x2, w, layer_idx, res, tm_pref=1024, tn_pref=512):
    m, k1 = x1.shape
    k2 = x2.shape[1]
    assert k1 == k2
    n = w.shape[-1]
    tm, tn = _tile(m, tm_pref, SUBLANE), _tile(n, tn_pref)
    vmem = 2 * (2 * tm * k1 * 2 + 2 * k1 * tn * 4 + 2 * tm * tn * 4) + 2 * k1 * tn * 2 + tm * tn * 4 + (4 << 20)
    return pl.pallas_call(
        _mm2_res_body, grid=(n // tn, m // tm),
        in_specs=[pl.BlockSpec((tm, k1), lambda j, i: (i, 0)), pl.BlockSpec((tm, k2), lambda j, i: (i, 0)),
                  _w_spec(w, layer_idx, k1, tn, 0), _w_spec(w, layer_idx, k2, tn, 1),
                  pl.BlockSpec((tm, tn), lambda j, i: (i, j))],
        out_specs=pl.BlockSpec((tm, tn), lambda j, i: (i, j)),
        out_shape=jax.ShapeDtypeStruct((m, n), F32),
        scratch_shapes=[pltpu.VMEM((k1, tn), BF16), pltpu.VMEM((k2, tn), BF16)],
        compiler_params=_cparams(("parallel", "arbitrary"), vmem), name="mm_wo")(x1, x2, w, w, res)


def _mm_ple_body(x_ref, w_ref, r_ref, e_ref, o_ref, w_sc):
    @pl.when(_first_inner_step())
    def _():
        w_sc[...] = w_ref[...].astype(BF16)

    acc = _dot(x_ref[...], w_sc[...])
    o_ref[...] = r_ref[...] + _sigmoid(acc) * e_ref[...].astype(F32)


def mm_ple(x, w, layer_idx, res, e, tm_pref=1024, tn_pref=512):
    m, k = x.shape
    n = w.shape[-1]
    tm, tn = _tile(m, tm_pref, SUBLANE), _tile(n, tn_pref)
    vmem = 2 * (tm * k * 2 + k * tn * 4 + 3 * tm * tn * 4) + k * tn * 2 + 2 * tm * tn * 4 + (4 << 20)
    return pl.pallas_call(
        _mm_ple_body, grid=(n // tn, m // tm),
        in_specs=[pl.BlockSpec((tm, k), lambda j, i: (i, 0)), _w_spec(w, layer_idx, k, tn),
                  pl.BlockSpec((tm, tn), lambda j, i: (i, j)), pl.BlockSpec((tm, tn), lambda j, i: (i, j))],
        out_specs=pl.BlockSpec((tm, tn), lambda j, i: (i, j)),
        out_shape=jax.ShapeDtypeStruct((m, n), F32),
        scratch_shapes=[pltpu.VMEM((k, tn), BF16)],
        compiler_params=_cparams(("parallel", "arbitrary"), vmem), name="mm_ple")(x, w, res, e)


def _mm_res_body(x_ref, w_ref, r_ref, o_ref, w_sc):
    @pl.when(_first_inner_step())
    def _():
        w_sc[...] = w_ref[...].astype(BF16)

    o_ref[...] = r_ref[...] + _dot(x_ref[...], w_sc[...])


def mm_down(x, w, layer_idx, res, tm_pref=512, tn_pref=512, tk_pref=5632):
    m, k = x.shape
    n = w.shape[-1]
    tm, tn, tk = _tile(m, tm_pref, SUBLANE), _tile(n, tn_pref), _tile(k, tk_pref)
    vmem = 2 * (tm * tk * 2 + tk * tn * 4 + 2 * tm * tn * 4) + tk * tn * 2 + tm * tn * 4 + (4 << 20)
    for kb in range(k // tk):
        res = pl.pallas_call(
            _mm_res_body, grid=(n // tn, m // tm),
            in_specs=[pl.BlockSpec((tm, tk), lambda j, i, kb=kb: (i, kb)), _w_spec(w, layer_idx, tk, tn, kb),
                      pl.BlockSpec((tm, tn), lambda j, i: (i, j))],
            out_specs=pl.BlockSpec((tm, tn), lambda j, i: (i, j)),
            out_shape=jax.ShapeDtypeStruct((m, n), F32),
            scratch_shapes=[pltpu.VMEM((tk, tn), BF16)],
            compiler_params=_cparams(("parallel", "arbitrary"), vmem), name="mm_down")(x, w, res)
    return res


def _ple_embed_body(p_ref, w_ref, g_ref, o_ref):
    e = _dot(p_ref[...].astype(BF16), w_ref[...].astype(BF16))
    o_ref[...] = _rms(e, g_ref[...]).astype(o_ref.dtype)


def ple_embed(p, w_pe, layer, g):
    t, kp = p.shape
    d = w_pe.shape[-1]
    tm = _tile(t, 512, SUBLANE)
    vmem = 2 * (tm * kp * 4 + kp * d * 4 + tm * d * 2) + 2 * tm * d * 4 + kp * d * 2 + (4 << 20)
    return pl.pallas_call(
        _ple_embed_body, grid=(t // tm,),
        in_specs=[pl.BlockSpec((tm, kp), lambda i: (i, 0)),
                  pl.BlockSpec((None, kp, d), lambda i: (layer, 0, 0)),
                  pl.BlockSpec((1, d), lambda i: (0, 0))],
        out_specs=pl.BlockSpec((tm, d), lambda i: (i, 0)),
        out_shape=jax.ShapeDtypeStruct((t, d), BF16),
        compiler_params=_cparams(("parallel",), vmem), name="ple_embed")(p, w_pe, g.reshape(1, d))


def _rotate(xg, tab):
    return xg * tab[:, 0:LANE] + pltpu.roll(xg, ROPE // 2, 1) * tab[:, LANE:2 * LANE]


def _qkv_post_body(z_ref, tab_ref, wuq_ref, wuk_ref, gqa_ref, gq_ref, gkva_ref, gk_ref, *outs,
                   nh, qlr, sample, q_scale):
    if sample:
        q_ref, ckv_ref, kr_ref = outs
    else:
        q_ref, ckv_ref, kr_ref, kk_ref = outs
    tm = z_ref.shape[0]
    tab = tab_ref[...]
    lane = lax.broadcasted_iota(jnp.int32, (tm, LANE), 1)
    lo = lane < ROPE

    qa = _rms(z_ref[:, 0:qlr], gqa_ref[...]).astype(BF16)
    q = _dot(qa, wuq_ref[...])
    gq = gq_ref[...]
    for h in range(nh):
        qn = q[:, h * QH:h * QH + NOPE]
        qr = q[:, h * QH + NOPE:(h + 1) * QH]
        ssq = jnp.sum(qn * qn, axis=-1, keepdims=True) + jnp.sum(jnp.where(lo, qr * qr, 0.0), axis=-1, keepdims=True)
        sc = lax.rsqrt(ssq * (1.0 / (NOPE + ROPE)) + RMS_EPS)
        qn = (qn * sc * gq[:, 0:NOPE]).astype(BF16)
        qlat = _dot(qn, wuk_ref[h]) * q_scale
        rot = jnp.where(lo, _rotate(qr * sc * gq[:, NOPE:QH], tab), 0.0) * q_scale
        if sample:
            q_ref[:, h, :, 0:KVR] = qlat.reshape(tm // SUBLANE, SUBLANE, KVR)
            q_ref[:, h, :, KVR:QW] = rot.reshape(tm // SUBLANE, SUBLANE, LANE)
        else:
            q_ref[h, :, 0:KVR] = qlat.astype(q_ref.dtype)
            q_ref[h, :, KVR:QW] = rot.astype(q_ref.dtype)

    c = _rms(z_ref[:, qlr:qlr + KVR], gkva_ref[...])
    ckv_ref[...] = c
    k1 = z_ref[:, qlr + KVR:qlr + KVR + LANE]
    k2 = jnp.where(lo, k1, pltpu.roll(k1, ROPE, 1))
    ssq = jnp.sum(jnp.where(lo, k2 * k2, 0.0), axis=-1, keepdims=True)
    kr = _rotate(k2 * lax.rsqrt(ssq * (1.0 / ROPE) + RMS_EPS) * gk_ref[...], tab)
    kr_ref[...] = kr[:, 0:ROPE]
    if not sample:
        kk_ref[:, 0:KVR] = c.astype(BF16)
        kk_ref[:, KVR:QW] = jnp.where(lo, kr, 0.0).astype(BF16)


def qkv_post(z1, row0, rows, tab, tab_period, wuq, wuk_t, gqa, gq, gkva, gk, nh, sample, q_scale):
    qlr = gqa.shape[-1]
    tm = _tile(rows, 256, 16) if not sample else _tile(rows, 128, SUBLANE)
    assert row0 % tm == 0 and tab_period % tm == 0
    rb0 = row0 // tm
    tb = tab_period // tm
    zw = z1.shape[1]
    in_specs = [pl.BlockSpec((tm, zw), lambda i: (rb0 + i, 0)),
                pl.BlockSpec((tm, 2 * LANE), lambda i: (i % tb, 0)),
                pl.BlockSpec(wuq.shape, lambda i: (0, 0)),
                pl.BlockSpec(wuk_t.shape, lambda i: (0, 0, 0)),
                pl.BlockSpec((1, qlr), lambda i: (0, 0)),
                pl.BlockSpec((1, QH), lambda i: (0, 0)),
                pl.BlockSpec((1, KVR), lambda i: (0, 0)),
                pl.BlockSpec((1, LANE), lambda i: (0, 0))]
    if sample:
        q_shape = jax.ShapeDtypeStruct((rows // SUBLANE, nh, SUBLANE, QW), F32)
        q_spec = pl.BlockSpec((tm // SUBLANE, nh, SUBLANE, QW), lambda i: (i, 0, 0, 0))
    else:
        q_shape = jax.ShapeDtypeStruct((nh, rows, QW), BF16)
        q_spec = pl.BlockSpec((nh, tm, QW), lambda i: (0, i, 0))
    out_shape = [q_shape, jax.ShapeDtypeStruct((rows, KVR), F32), jax.ShapeDtypeStruct((rows, ROPE), F32)]
    out_specs = [q_spec, pl.BlockSpec((tm, KVR), lambda i: (i, 0)), pl.BlockSpec((tm, ROPE), lambda i: (i, 0))]
    if not sample:
        out_shape.append(jax.ShapeDtypeStruct((rows, QW), BF16))
        out_specs.append(pl.BlockSpec((tm, QW), lambda i: (i, 0)))
    vmem = 2 * (tm * zw * 4 + wuq.size * 2 + wuk_t.size * 2 + nh * tm * QW * 4) + 3 * tm * nh * QH * 4 + (8 << 20)
    return pl.pallas_call(
        functools.partial(_qkv_post_body, nh=nh, qlr=qlr, sample=sample, q_scale=q_scale),
        grid=(rows // tm,), in_specs=in_specs, out_specs=out_specs, out_shape=out_shape,
        compiler_params=_cparams(("parallel",), vmem), name="qkv_post_s" if sample else "qkv_post_p",
    )(z1, tab, wuq, wuk_t, gqa.reshape(1, qlr), gq, gkva.reshape(1, KVR), gk)


def _attn_p_body(q_ref, kk_ref, wuv_ref, gao_ref, o_ref, m_sc, l_sc, acc_sc, *, nh, tq, tk):
    qi = pl.program_id(1)
    ki = pl.program_id(2)
    last_k = (qi * tq + tq - 1) // tk
    rows = nh * tq

    @pl.when(ki == 0)
    def _():
        m_sc[...] = jnp.full(m_sc.shape, -jnp.inf, F32)
        l_sc[...] = jnp.zeros(l_sc.shape, F32)
        acc_sc[...] = jnp.zeros(acc_sc.shape, F32)

    def update(masked):
        q = q_ref[...].reshape(rows, QW)
        k = kk_ref[...]
        s = _dot_nt(q, k)
        if masked:
            qpos = (lax.broadcasted_iota(jnp.int32, (rows, tk), 0) & (tq - 1)) + qi * tq
            kpos = lax.broadcasted_iota(jnp.int32, (rows, tk), 1) + ki * tk
            s = jnp.where(kpos <= qpos, s, -jnp.inf)
        m_prev = m_sc[...]
        m_new = jnp.maximum(m_prev, jnp.max(s, axis=-1, keepdims=True))
        alpha = jnp.exp2(m_prev - m_new)
        p = jnp.exp2(s - m_new)
        l_sc[...] = alpha * l_sc[...] + jnp.sum(p, axis=-1, keepdims=True)
        acc_sc[...] = alpha * acc_sc[...] + _dot(p.astype(BF16), k[:, 0:KVR])
        m_sc[...] = m_new

    @pl.when(ki < last_k)
    def _():
        update(False)

    @pl.when(ki == last_k)
    def _():
        update(True)
        inv = 1.0 / l_sc[...]
        for h in range(nh):
            sl = slice(h * tq, (h + 1) * tq)
            o = (acc_sc[sl, :] * inv[sl, :]).astype(BF16)
            v = _dot(o, wuv_ref[h])
            o_ref[:, h * VH:(h + 1) * VH] = _rms(v, gao_ref[h]).astype(o_ref.dtype)


def attn_prompt(q, kk, wuv_t, g_ao, b, s, nh):
    tq = _tile(s, 128, 16)
    tk = _tile(s, 512, 16)
    assert tq & (tq - 1) == 0 and tk % tq == 0
    nq, nk = s // tq, s // tk
    rows = nh * tq

    def kk_map(bi, qi, ki):
        return (bi * nk + jnp.minimum(ki, (qi * tq + tq - 1) // tk), 0)

    vmem = 2 * (rows * QW * 2 + tk * QW * 2 + tq * nh * VH * 2 + wuv_t.size * 2) + rows * (KVR + 2 * LANE) * 4 \
        + 4 * rows * tk * 4 + (8 << 20)
    return pl.pallas_call(
        functools.partial(_attn_p_body, nh=nh, tq=tq, tk=tk),
        grid=(b, nq, nk),
        in_specs=[pl.BlockSpec((nh, tq, QW), lambda bi, qi, ki: (0, bi * nq + qi, 0)),
                  pl.BlockSpec((tk, QW), kk_map),
                  pl.BlockSpec(wuv_t.shape, lambda bi, qi, ki: (0, 0, 0)),
                  pl.BlockSpec(g_ao.shape, lambda bi, qi, ki: (0, 0, 0))],
        out_specs=pl.BlockSpec((tq, nh * VH), lambda bi, qi, ki: (bi * nq + qi, 0)),
        out_shape=jax.ShapeDtypeStruct((b * s, nh * VH), BF16),
        scratch_shapes=[pltpu.VMEM((rows, 1), F32), pltpu.VMEM((rows, 1), F32), pltpu.VMEM((rows, KVR), F32)],
        compiler_params=_cparams(("parallel", "parallel", "arbitrary"), vmem), name="attn_prompt",
    )(q, kk, wuv_t, g_ao)


def _attn_s_body(pt_ref, q_ref, cn_ref, kn_ref, *rest, nh, ds, gp, n_chain):
    c_refs = rest[0:gp]
    k_refs = rest[gp:2 * gp]
    o_ref, m_sc, l_sc, acc_sc = rest[2 * gp:]
    g = pl.program_id(1)
    rows = nh * ds
    q = q_ref[...].reshape(rows, QW).astype(BF16)
    q_lat = q[:, 0:KVR]
    q_rope = q[:, KVR:KVR + ROPE]

    def update(ch, s, v):
        m_prev = m_sc[ch]
        m_new = jnp.maximum(m_prev, jnp.max(s, axis=-1, keepdims=True))
        alpha = jnp.exp2(m_prev - m_new)
        p = jnp.exp2(s - m_new)
        l_sc[ch] = alpha * l_sc[ch] + jnp.sum(p, axis=-1, keepdims=True)
        acc_sc[ch] = alpha * acc_sc[ch] + _dot(p.astype(BF16), v)
        m_sc[ch] = m_new

    @pl.when(g == 0)
    def _():
        m_sc[...] = jnp.full(m_sc.shape, -jnp.inf, F32)
        l_sc[...] = jnp.zeros(l_sc.shape, F32)
        acc_sc[...] = jnp.zeros(acc_sc.shape, F32)
        pad = LANE - ds
        c0 = jnp.concatenate([cn_ref[...], jnp.zeros((pad, KVR), F32)], axis=0).astype(BF16)
        k0 = jnp.concatenate([kn_ref[...], jnp.zeros((pad, ROPE), F32)], axis=0).astype(BF16)
        s = _dot_nt(q_lat, c0) + _dot_nt(q_rope, k0)
        tpos = lax.broadcasted_iota(jnp.int32, (rows, LANE), 0) & (ds - 1)
        kpos = lax.broadcasted_iota(jnp.int32, (rows, LANE), 1)
        update(0, jnp.where(kpos <= tpos, s, -jnp.inf), c0)

    per = gp // n_chain
    for ch in range(n_chain):
        c = jnp.concatenate([r[...] for r in c_refs[ch * per:(ch + 1) * per]], axis=0).astype(BF16)
        kt = jnp.concatenate([r[...] for r in k_refs[ch * per:(ch + 1) * per]], axis=1).astype(BF16)
        update(ch, _dot_nt(q_lat, c) + _dot(q_rope, kt), c)

    @pl.when(g == pl.num_programs(1) - 1)
    def _():
        m = m_sc[0]
        for ch in range(1, n_chain):
            m = jnp.maximum(m, m_sc[ch])
        l = jnp.zeros((rows, 1), F32)
        acc = jnp.zeros((rows, KVR), F32)
        for ch in range(n_chain):
            w = jnp.exp2(m_sc[ch] - m)
            l = l + w * l_sc[ch]
            acc = acc + w * acc_sc[ch]
        o_ref[...] = (acc / l).reshape(nh, ds, KVR)


def attn_sample(q, c_new, k_new, cache_ckv, cache_krope_t, page_table, layer, nh):
    db, _, ds, _ = q.shape
    n_pages = page_table.shape[1]
    page = cache_ckv.shape[2]
    assert ds == SUBLANE and ds & (ds - 1) == 0
    gp = next(g for g in (32, 16, 8, 4, 2, 1) if n_pages % g == 0)
    n_chain = 2 if gp % 2 == 0 else 1
    ng = n_pages // gp
    rows = nh * ds
    pt = page_table.reshape(-1)

    def page_spec(shape, i):
        return pl.BlockSpec((None, None) + shape,
                            lambda b, g, pt_ref: (layer, pt_ref[b * n_pages + g * gp + i], 0, 0))

    in_specs = [pl.BlockSpec((None, nh, ds, QW), lambda b, g, pt_ref: (b, 0, 0, 0)),
                pl.BlockSpec((ds, KVR), lambda b, g, pt_ref: (b, 0)),
                pl.BlockSpec((ds, ROPE), lambda b, g, pt_ref: (b, 0))]
    in_specs += [page_spec((page, KVR), i) for i in range(gp)] + [page_spec((ROPE, page), i) for i in range(gp)]
    grid_spec = pltpu.PrefetchScalarGridSpec(
        num_scalar_prefetch=1, grid=(db, ng), in_specs=in_specs,
        out_specs=pl.BlockSpec((None, nh, ds, KVR), lambda b, g, pt_ref: (b, 0, 0, 0)),
        scratch_shapes=[pltpu.VMEM((n_chain, rows, 1), F32), pltpu.VMEM((n_chain, rows, 1), F32),
                        pltpu.VMEM((n_chain, rows, KVR), F32)])
    keys = gp * page
    vmem = 2 * keys * (KVR + ROPE) * 4 + 3 * keys * (KVR + ROPE) * 2 + 4 * rows * keys * 4 + (8 << 20)
    return pl.pallas_call(
        functools.partial(_attn_s_body, nh=nh, ds=ds, gp=gp, n_chain=n_chain),
        grid_spec=grid_spec, out_shape=jax.ShapeDtypeStruct((db, nh, ds, KVR), F32),
        compiler_params=_cparams(("parallel", "arbitrary"), vmem), name="attn_sample",
    )(pt, q, c_new, k_new, *([cache_ckv] * gp), *([cache_krope_t] * gp))


def _ouv_body(o_ref, w_ref, g_ref, out_ref):
    bx, by, _ = o_ref.shape
    o = o_ref[...].reshape(bx * by, KVR).astype(BF16)
    v = _dot(o, w_ref[...])
    out_ref[...] = _rms(v, g_ref[...]).astype(out_ref.dtype)


def ouv(o_lat, wuv_t, g_ao, bx, by):
    x, nh, y, _ = o_lat.shape
    tm = bx * by
    nyb = y // by
    vmem = 2 * (tm * KVR * 4 + KVR * VH * 2 + tm * VH * 2) + 4 * tm * KVR * 4 + (4 << 20)
    return pl.pallas_call(
        _ouv_body, grid=(x // bx, nyb, nh),
        in_specs=[pl.BlockSpec((bx, None, by, KVR), lambda xi, yi, h: (xi, h, yi, 0)),
                  pl.BlockSpec((None, KVR, VH), lambda xi, yi, h: (h, 0, 0)),
                  pl.BlockSpec((None, 1, VH), lambda xi, yi, h: (h, 0, 0))],
        out_specs=pl.BlockSpec((tm, VH), lambda xi, yi, h: (xi * nyb + yi, h)),
        out_shape=jax.ShapeDtypeStruct((x * y, nh * VH), BF16),
        compiler_params=_cparams(("parallel", "parallel", "arbitrary"), vmem), name="ouv")(o_lat, wuv_t, g_ao)


def _conv_core(gb, gc, xc, p0, p1, period, w_ref, g_ref, y_ref, groups):
    tm, c = gb.shape
    u = gc.astype(F32) * xc.astype(F32)
    t = lax.broadcasted_iota(jnp.int32, (tm, c), 0) & (period - 1)
    um1 = jnp.where(t == 0, p1, pltpu.roll(u, 1, 0))
    um2 = jnp.where(t == 0, p0, jnp.where(t == 1, p1, pltpu.roll(u, 2, 0)))
    w = w_ref[...]
    y = um2 * w[0:1, :] + um1 * w[1:2, :] + u * w[2:3, :]
    y = gb.astype(F32) * y
    gw = c // groups
    g = g_ref[...]
    for i in range(groups):
        sl = slice(i * gw, (i + 1) * gw)
        y_ref[:, sl] = _rms(y[:, sl], g[:, sl]).astype(y_ref.dtype)
    return u


def _conv_p_body(gb_ref, gc_ref, xc_ref, w_ref, g_ref, y_ref, st_ref, prev_sc, *, groups):
    ts = gb_ref.shape[0]

    @pl.when(pl.program_id(1) == 0)
    def _():
        prev_sc[...] = jnp.zeros(prev_sc.shape, F32)

    u = _conv_core(gb_ref[...], gc_ref[...], xc_ref[...], prev_sc[0:1, :], prev_sc[1:2, :], ts,
                   w_ref, g_ref, y_ref, groups)
    tail = u[ts - 2:ts, :]
    prev_sc[0:2, :] = tail
    st_ref[...] = tail


def conv_prompt(z2, b, s, conv_w, g_co, groups):
    c = conv_w.shape[-1]
    ts = _tile(s, 256, 16)
    assert ts & (ts - 1) == 0 and conv_w.shape[0] == 3
    ns = s // ts

    def col(j):
        return pl.BlockSpec((ts, c), lambda bi, si: (bi * ns + si, j))

    vmem = 2 * (3 * ts * c * 2 + ts * c * 2) + 8 * ts * c * 4 + (4 << 20)
    return pl.pallas_call(
        functools.partial(_conv_p_body, groups=groups), grid=(b, ns),
        in_specs=[col(0), col(1), col(2), pl.BlockSpec((3, c), lambda bi, si: (0, 0)),
                  pl.BlockSpec((1, c), lambda bi, si: (0, 0))],
        out_specs=[pl.BlockSpec((ts, c), lambda bi, si: (bi * ns + si, 0)),
                   pl.BlockSpec((None, 2, c), lambda bi, si: (bi, 0, 0))],
        out_shape=[jax.ShapeDtypeStruct((b * s, c), BF16), jax.ShapeDtypeStruct((b, 2, c), F32)],
        scratch_shapes=[pltpu.VMEM((SUBLANE, c), F32)],
        compiler_params=_cparams(("parallel", "arbitrary"), vmem), name="conv_prompt",
    )(z2, z2, z2, conv_w, g_co.reshape(1, c))


def _conv_s_body(gb_ref, gc_ref, xc_ref, st_ref, w_ref, g_ref, y_ref, new_ref, *, groups, ds):
    tm, c = gb_ref.shape
    bb = tm // ds
    st = st_ref[...]
    p0 = jnp.broadcast_to(st[:, 0:1, :], (bb, ds, c)).reshape(tm, c)
    p1 = jnp.broadcast_to(st[:, 1:2, :], (bb, ds, c)).reshape(tm, c)
    u = _conv_core(gb_ref[...], gc_ref[...], xc_ref[...], p0, p1, ds, w_ref, g_ref, y_ref, groups)
    new_ref[...] = u.reshape(bb, ds, c)[:, ds - 2:ds, :]


def conv_sample(z2, row0, db, ds, state, conv_w, g_co, groups):
    c = conv_w.shape[-1]
    assert ds == SUBLANE
    bb = _tile(db, 16, 1)
    tm = bb * ds
    assert row0 % tm == 0
    rb0 = row0 // tm

    def col(j):
        return pl.BlockSpec((tm, c), lambda i: (rb0 + i, j))

    vmem = 2 * (3 * tm * c * 2 + tm * c * 2 + 2 * bb * SUBLANE * c * 4) + 10 * tm * c * 4 + (4 << 20)
    return pl.pallas_call(
        functools.partial(_conv_s_body, groups=groups, ds=ds), grid=(db // bb,),
        in_specs=[col(0), col(1), col(2), pl.BlockSpec((bb, 2, c), lambda i: (i, 0, 0)),
                  pl.BlockSpec((3, c), lambda i: (0, 0)), pl.BlockSpec((1, c), lambda i: (0, 0))],
        out_specs=[pl.BlockSpec((tm, c), lambda i: (i, 0)), pl.BlockSpec((bb, 2, c), lambda i: (i, 0, 0))],
        out_shape=[jax.ShapeDtypeStruct((db * ds, c), BF16), jax.ShapeDtypeStruct((db, 2, c), F32)],
        compiler_params=_cparams(("parallel",), vmem), name="conv_sample",
    )(z2, z2, z2, state, conv_w, g_co.reshape(1, c))


def _gather_body(idx_ref, valid_ref, src_ref, o_ref, buf, sem, *, rows):
    i = pl.program_id(0)

    def row_copy(r, src_row):
        return pltpu.make_async_copy(src_ref.at[pl.ds(src_row, 1)], buf.at[pl.ds(r, 1)], sem)

    @pl.when(valid_ref[i] != 0)
    def _():
        def issue(r, carry):
            row_copy(r, idx_ref[i * rows + r]).start()
            return carry

        lax.fori_loop(0, rows, issue, 0)

        def wait(r, carry):
            row_copy(r, 0).wait()
            return carry

        lax.fori_loop(0, rows, wait, 0)
        o_ref[...] = buf[...].astype(o_ref.dtype)

    @pl.when(valid_ref[i] == 0)
    def _():
        o_ref[...] = jnp.zeros(o_ref.shape, o_ref.dtype)


def gather_rows(src, idx, tile_valid, rows):
    p = idx.shape[0]
    d = src.shape[1]
    grid_spec = pltpu.PrefetchScalarGridSpec(
        num_scalar_prefetch=2, grid=(p // rows,),
        in_specs=[pl.BlockSpec(memory_space=pl.ANY)],
        out_specs=pl.BlockSpec((rows, d), lambda i, idx_ref, v_ref: (i, 0)),
        scratch_shapes=[pltpu.VMEM((rows, d), src.dtype), pltpu.SemaphoreType.DMA(())])
    vmem = rows * d * 4 + 2 * rows * d * 2 + rows * d * 4 + (4 << 20)
    return pl.pallas_call(
        functools.partial(_gather_body, rows=rows), grid_spec=grid_spec,
        out_shape=jax.ShapeDtypeStruct((p, d), BF16),
        compiler_params=_cparams(("arbitrary",), vmem), name="moe_gather")(idx, tile_valid, src)


def _combine_body(p0_ref, p1_ref, ys_ref, h_ref, o_ref, b0, b1, sem, *, rows):
    i = pl.program_id(0)

    def row_copy(r, src_row, buf):
        return pltpu.make_async_copy(ys_ref.at[pl.ds(src_row, 1)], buf.at[pl.ds(r, 1)], sem)

    def issue(r, carry):
        row_copy(r, p0_ref[i * rows + r], b0).start()
        row_copy(r, p1_ref[i * rows + r], b1).start()
        return carry

    lax.fori_loop(0, rows, issue, 0)

    def wait(r, carry):
        row_copy(r, 0, b0).wait()
        row_copy(r, 0, b1).wait()
        return carry

    lax.fori_loop(0, rows, wait, 0)
    o_ref[...] = h_ref[...] + b0[...] + b1[...]


def moe_combine(ys, pos0, pos1, h, rows=256):
    t, d = h.shape
    rows = _tile(t, rows, SUBLANE)
    grid_spec = pltpu.PrefetchScalarGridSpec(
        num_scalar_prefetch=2, grid=(t // rows,),
        in_specs=[pl.BlockSpec(memory_space=pl.ANY), pl.BlockSpec((rows, d), lambda i, a, b: (i, 0))],
        out_specs=pl.BlockSpec((rows, d), lambda i, a, b: (i, 0)),
        scratch_shapes=[pltpu.VMEM((rows, d), F32), pltpu.VMEM((rows, d), F32), pltpu.SemaphoreType.DMA(())])
    vmem = 6 * rows * d * 4 + rows * d * 4 + (4 << 20)
    return pl.pallas_call(
        functools.partial(_combine_body, rows=rows), grid_spec=grid_spec,
        out_shape=jax.ShapeDtypeStruct((t, d), F32),
        compiler_params=_cparams(("arbitrary",), vmem), name="moe_combine")(pos0, pos1, ys, h)


def _new_expert(te_ref):
    i = pl.program_id(1)
    return jnp.logical_or(i == 0, te_ref[i] != te_ref[jnp.maximum(i - 1, 0)])


def _by_row_count(nrows, o_ref, fn):
    tm = o_ref.shape[0]
    half = tm // 2

    @pl.when(nrows > half)
    def _():
        o_ref[...] = fn(slice(0, tm))

    @pl.when(jnp.logical_and(nrows > 0, nrows <= half))
    def _():
        o_ref[0:half, :] = fn(slice(0, half))
        o_ref[half:tm, :] = jnp.zeros((tm - half, o_ref.shape[1]), o_ref.dtype)

    @pl.when(nrows == 0)
    def _():
        o_ref[...] = jnp.zeros(o_ref.shape, o_ref.dtype)


def _moe_up_body(te_ref, tb_ref, tv_ref, x_ref, wg_ref, wu_ref, o_ref, wg_sc, wu_sc):
    valid = tv_ref[pl.program_id(1)] != 0

    @pl.when(jnp.logical_and(valid, _new_expert(te_ref)))
    def _():
        wg_sc[...] = wg_ref[...].astype(BF16)
        wu_sc[...] = wu_ref[...].astype(BF16)

    def swiglu(x):
        g = _dot(x, wg_sc[...])
        u = _dot(x, wu_sc[...])
        return (g * _sigmoid(g) * u).astype(o_ref.dtype)

    _by_row_count(tv_ref[pl.program_id(1)], o_ref, lambda sl: swiglu(x_ref[sl, :]))


def moe_up_call(xs, wg, wu, j, tile_expert, tile_block, tile_valid, tm, tf_pref=512):
    p, k = xs.shape
    f = wg.shape[-1]
    tf = _tile(f, tf_pref)

    def w_map(fi, i, te, tb, tv):
        return (j, te[i], 0, fi)

    grid_spec = pltpu.PrefetchScalarGridSpec(
        num_scalar_prefetch=3, grid=(f // tf, p // tm),
        in_specs=[pl.BlockSpec((tm, k), lambda fi, i, te, tb, tv: (tb[i], 0)),
                  pl.BlockSpec((None, None, k, tf), w_map), pl.BlockSpec((None, None, k, tf), w_map)],
        out_specs=pl.BlockSpec((tm, tf), lambda fi, i, te, tb, tv: (i, fi)),
        scratch_shapes=[pltpu.VMEM((k, tf), BF16), pltpu.VMEM((k, tf), BF16)])
    vmem = 2 * (tm * k * 2 + 2 * k * tf * 4 + tm * tf * 2) + 2 * k * tf * 2 + 3 * tm * tf * 4 + (4 << 20)
    return pl.pallas_call(
        _moe_up_body, grid_spec=grid_spec, out_shape=jax.ShapeDtypeStruct((p, f), BF16),
        compiler_params=_cparams(("parallel", "arbitrary"), vmem), name="moe_up",
    )(tile_expert, tile_block, tile_valid, xs, wg, wu)


def _moe_down_body(te_ref, tb_ref, tv_ref, a_ref, w_ref, g_ref, o_ref, w_sc):
    valid = tv_ref[pl.program_id(1)] != 0

    @pl.when(jnp.logical_and(valid, _new_expert(te_ref)))
    def _():
        w_sc[...] = w_ref[...].astype(BF16)

    _by_row_count(tv_ref[pl.program_id(1)], o_ref, lambda sl: _dot(a_ref[sl, :], w_sc[...]) * g_ref[sl, :])


def moe_down_call(act, wd, j, row_gate, tile_expert, tile_block, tile_valid, tm, tn_pref=512):
    p, k = act.shape
    n = wd.shape[-1]
    tn = _tile(n, tn_pref)
    grid_spec = pltpu.PrefetchScalarGridSpec(
        num_scalar_prefetch=3, grid=(n // tn, p // tm),
        in_specs=[pl.BlockSpec((tm, k), lambda ni, i, te, tb, tv: (tb[i], 0)),
                  pl.BlockSpec((None, None, k, tn), lambda ni, i, te, tb, tv: (j, te[i], 0, ni)),
                  pl.BlockSpec((tm, 1), lambda ni, i, te, tb, tv: (tb[i], 0))],
        out_specs=pl.BlockSpec((tm, tn), lambda ni, i, te, tb, tv: (i, ni)),
        scratch_shapes=[pltpu.VMEM((k, tn), BF16)])
    vmem = 2 * (tm * k * 2 + k * tn * 4 + tm * tn * 4 + tm * LANE * 4) + k * tn * 2 + tm * tn * 4 + (4 << 20)
    return pl.pallas_call(
        _moe_down_body, grid_spec=grid_spec, out_shape=jax.ShapeDtypeStruct((p, n), F32),
        compiler_params=_cparams(("parallel", "arbitrary"), vmem), name="moe_down",
    )(tile_expert, tile_block, tile_valid, act, wd, row_gate)


def moe_route(logits, n_exp, tm):
    t = logits.shape[0]
    top_val, top_idx = lax.top_k(logits[:, :n_exp], TOP_K)
    wts = jax.nn.softmax(top_val, axis=-1)
    e_flat = top_idx.T.reshape(-1).astype(jnp.int32)
    g_flat = wts.T.reshape(-1)
    tok = jnp.tile(jnp.arange(t, dtype=jnp.int32), TOP_K)
    counts = jnp.sum(jax.nn.one_hot(e_flat, n_exp, dtype=jnp.int32), axis=0)
    padded = ((counts + tm - 1) // tm) * tm
    pad_end = jnp.cumsum(padded)
    pad_start = pad_end - padded
    start = jnp.cumsum(counts) - counts
    order = jnp.argsort(e_flat, stable=True)
    e_sorted = e_flat[order]
    dest_sorted = pad_start[e_sorted] + jnp.arange(TOP_K * t, dtype=jnp.int32) - start[e_sorted]
    p = TOP_K * t + n_exp * tm
    row_token = jnp.zeros((p,), jnp.int32).at[dest_sorted].set(tok[order])
    row_gate = jnp.zeros((p,), F32).at[dest_sorted].set(g_flat[order])
    dest = jnp.zeros((TOP_K * t,), jnp.int32).at[order].set(dest_sorted)
    nt = p // tm
    tile_start = jnp.arange(nt, dtype=jnp.int32) * tm
    total = pad_end[-1]
    last_tile = total // tm - 1
    tile_block = jnp.minimum(jnp.arange(nt, dtype=jnp.int32), last_tile).astype(jnp.int32)
    tile_expert = jnp.minimum(jnp.searchsorted(pad_end, tile_block * tm, side="right"), n_exp - 1).astype(jnp.int32)
    real_end = pad_start + counts
    tile_rows = jnp.where(tile_start < total, jnp.clip(real_end[tile_expert] - tile_start, 0, tm), 0).astype(jnp.int32)
    return row_token, row_gate.reshape(p, 1), dest[:t], dest[t:], tile_expert, tile_block, tile_rows


def _rope_table(pos):
    half = ROPE // 2
    inv_freq = jnp.exp(-math.log(ROPE_THETA) * jnp.arange(half, dtype=F32) / half)
    ang = pos.astype(F32)[:, None] * inv_freq[None, :]
    cos, sin = jnp.cos(ang), jnp.sin(ang)
    return jnp.concatenate([cos, cos, cos, cos, -sin, sin, -sin, sin], axis=-1)


def kernel(x_prompt, x_sample, p_prompt, p_sample, cache_ckv, cache_krope, state_conv, page_table, g_mix, w_in, g_qa, w_uq, g_qn, g_kva, g_kn, w_uk, w_uv, conv_w, g_ao, g_co, w_o, g_ffn, w_ffn_gate, w_ffn_up, w_ffn_down, w_router, moe_gate, moe_up, moe_down, w_pe, g_pe, g_pg, w_pg):
    b, s, d = x_prompt.shape
    db, ds, _ = x_sample.shape
    depth = w_in.shape[0]
    qlr = g_qa.shape[-1]
    nh = w_uq.shape[2]
    c_dim = conv_w.shape[-1]
    groups = g_co.shape[1]
    n_exp = w_router.shape[-1]
    assert w_uq.shape[3] == NOPE + ROPE and g_kva.shape[-1] == KVR and g_kn.shape[-1] == ROPE
    assert w_uv.shape[-1] == VH and cache_ckv.shape[-1] == KVR and cache_krope.shape[-1] == ROPE
    assert c_dim // groups == LANE and nh * VH == c_dim
    tp, tsmp = b * s, db * ds
    past_len = page_table.shape[1] * cache_ckv.shape[2]
    q_scale = float((NOPE + ROPE) ** -0.5) * LOG2E
    c0 = qlr + KVR + ROPE
    z1w = qlr + KVR + LANE
    moe_tm = 512

    h = jnp.concatenate([x_prompt.reshape(tp, d), x_sample.reshape(tsmp, d)], axis=0)
    p_all = jnp.concatenate([p_prompt.reshape(depth, tp, -1), p_sample.reshape(depth, tsmp, -1)], axis=1)
    tab_p = _rope_table(jnp.arange(s))
    tab_s = jnp.tile(_rope_table(past_len + jnp.arange(ds)), (LANE // ds, 1))
    w_in_t = jnp.swapaxes(w_in, 1, 2)
    cache_krope_t = jnp.swapaxes(cache_krope, 2, 3)

    outs = [[] for _ in range(6)]
    for i in range(depth):
        wq = w_uq[i]
        wuq = jnp.concatenate([wq, wq[..., NOPE:]], axis=-1).reshape(qlr, nh * QH).astype(BF16)
        wuk_t = jnp.transpose(w_uk[i], (1, 2, 0)).astype(BF16)
        wuv_t = jnp.transpose(w_uv[i], (1, 0, 2)).astype(BF16)
        gq = jnp.concatenate([g_qn[i], g_qn[i, NOPE:]]).reshape(1, QH)
        gk = jnp.concatenate([g_kn[i], g_kn[i]]).reshape(1, LANE)
        g_ao_i = g_ao[i].reshape(nh, 1, VH)

        (a,) = rmsnorm(h, g_mix[i])
        z1 = mm_nt(a, w_in_t, i, 0, z1w, F32, tn_pref=384)
        z2 = mm_nt(a, w_in_t, i, c0, 3 * c_dim, BF16)
        q_p, ckv_p, kr_p, kk_p = qkv_post(z1, 0, tp, tab_p, s, wuq, wuk_t, g_qa[i], gq, g_kva[i], gk,
                                          nh, False, q_scale)
        q_s, ckv_s, kr_s = qkv_post(z1, tp, tsmp, tab_s, tab_s.shape[0], wuq, wuk_t, g_qa[i], gq, g_kva[i], gk,
                                    nh, True, q_scale)
        o_p = attn_prompt(q_p, kk_p, wuv_t, g_ao_i, b, s, nh)
        ol_s = attn_sample(q_s, ckv_s, kr_s, cache_ckv, cache_krope_t, page_table, i, nh)
        o_s = ouv(ol_s, wuv_t, g_ao_i, _tile(db, 64, 1), ds)
        yc_p, cv_p = conv_prompt(z2, b, s, conv_w[i], g_co[i], groups)
        yc_s, cv_s = conv_sample(z2, tp, db, ds, state_conv[i], conv_w[i], g_co[i], groups)
        o_all = jnp.concatenate([o_p, o_s], axis=0)
        yc_all = jnp.concatenate([yc_p, yc_s], axis=0)
        h = mm2_res(o_all, yc_all, w_o, (i,), h)

        j = i // 2
        if i % 2 == 0:
            (f,) = rmsnorm(h, g_ffn[i])
            act = mm_swiglu(f, w_ffn_gate, w_ffn_up, (j,))
            h = mm_down(act, w_ffn_down, (j,), h)
        else:
            f, logits = rmsnorm(h, g_ffn[i], router_w=w_router[j], y_dtype=F32)
            row_token, row_gate, pos0, pos1, t_exp, t_blk, t_val = moe_route(logits, n_exp, moe_tm)
            xs = gather_rows(f, row_token, t_val, moe_tm)
            act = moe_up_call(xs, moe_gate, moe_up, j, t_exp, t_blk, t_val, moe_tm)
            ys = moe_down_call(act, moe_down, j, row_gate, t_exp, t_blk, t_val, moe_tm)
            h = moe_combine(ys, pos0, pos1, h)

        (hn,) = rmsnorm(h, g_pg[i])
        e = ple_embed(p_all[i], w_pe, i, g_pe[i])
        h = mm_ple(hn, w_pg, (i,), h, e)

        for lst, v in zip(outs, (ckv_p, kr_p, cv_p, ckv_s, kr_s, cv_s)):
            lst.append(v)

    ckv_prompt = jnp.stack(outs[0]).reshape(depth, b, s, KVR)
    krope_prompt = jnp.stack(outs[1]).reshape(depth, b, s, ROPE)
    conv_prompt_out = jnp.stack(outs[2])
    ckv_sample = jnp.stack(outs[3]).reshape(depth, db, ds, KVR)
    krope_sample = jnp.stack(outs[4]).reshape(depth, db, ds, ROPE)
    conv_sample_out = jnp.stack(outs[5])
    return (h[:tp].reshape(b, s, d), h[tp:].reshape(db, ds, d), ckv_prompt, krope_prompt, conv_prompt_out,
            ckv_sample, krope_sample, conv_sample_out)
```

```python
import functools
import math

import jax
import jax.numpy as jnp
from jax import lax
from jax.experimental import pallas as pl
from jax.experimental.pallas import tpu as pltpu

F32 = jnp.float32
BF16 = jnp.bfloat16
RMS_EPS = 1e-6
ROPE_THETA = 10000.0
TOP_K = 2
LOG2E = math.log2(math.e)

V7X_VMEM_BYTES = 64 * 1024 * 1024
VMEM_BUDGET = V7X_VMEM_BYTES - 8 * 1024 * 1024
LANE = 128
SUBLANE = 8

NOPE = 128
ROPE = 64
QH = NOPE + 2 * ROPE
KVR = 256
QW = KVR + LANE
VH = 128


def _cparams(sem, vmem_bytes):
    return pltpu.CompilerParams(dimension_semantics=sem,
                                vmem_limit_bytes=int(min(max(vmem_bytes, 16 << 20), VMEM_BUDGET)))


def _tile(n, pref, align=LANE):
    if n <= pref:
        return n
    t = (pref // align) * align
    while t >= align:
        if n % t == 0:
            return t
        t -= align
    return n


def _dot(a, b):
    return jnp.dot(a, b, preferred_element_type=F32)


def _dot_nt(a, b):
    return lax.dot_general(a, b, (((1,), (1,)), ((), ())), preferred_element_type=F32)


def _dot_tn(a, b):
    return lax.dot_general(a, b, (((0,), (0,)), ((), ())), preferred_element_type=F32)


def _rms(x, g):
    ms = jnp.mean(x * x, axis=-1, keepdims=True)
    return x * lax.rsqrt(ms + RMS_EPS) * g


def _sigmoid(x):
    return 1.0 / (1.0 + jnp.exp(-x))


def _first_inner_step():
    return pl.program_id(1) == 0


def _norm_body(*refs, n_add, router, y_dtype):
    x_ref = refs[0]
    add_refs = refs[1:1 + n_add]
    g_ref = refs[1 + n_add]
    pos = 2 + n_add
    wr_ref = refs[pos] if router else None
    pos += 1 if router else 0
    outs = refs[pos:]
    x = x_ref[...]
    for a in add_refs:
        x = x + a[...]
    oi = 0
    if n_add:
        outs[oi][...] = x
        oi += 1
    y = _rms(x, g_ref[...])
    outs[oi][...] = y.astype(y_dtype)
    oi += 1
    if router:
        outs[oi][...] = jnp.dot(y, wr_ref[...], preferred_element_type=F32,
                                precision=lax.Precision.HIGHEST)


def rmsnorm(x, g, adds=(), router_w=None, y_dtype=BF16):
    t, d = x.shape
    tm = _tile(t, 256, SUBLANE)
    n_add = len(adds)
    row = pl.BlockSpec((tm, d), lambda i: (i, 0))
    in_specs = [row] * (1 + n_add) + [pl.BlockSpec((1, d), lambda i: (0, 0))]
    args = [x, *adds, g.reshape(1, d)]
    out_shape, out_specs = [], []
    if n_add:
        out_shape.append(jax.ShapeDtypeStruct((t, d), F32))
        out_specs.append(row)
    out_shape.append(jax.ShapeDtypeStruct((t, d), y_dtype))
    out_specs.append(row)
    if router_w is not None:
        ne = router_w.shape[1]
        wr = jnp.zeros((d, LANE), F32).at[:, :ne].set(router_w)
        in_specs.append(pl.BlockSpec((d, LANE), lambda i: (0, 0)))
        args.append(wr)
        out_shape.append(jax.ShapeDtypeStruct((t, LANE), F32))
        out_specs.append(pl.BlockSpec((tm, LANE), lambda i: (i, 0)))
    vmem = 2 * tm * d * 4 * (2 + 2 * n_add) + (4 << 20)
    outs = pl.pallas_call(
        functools.partial(_norm_body, n_add=n_add, router=router_w is not None, y_dtype=y_dtype),
        grid=(t // tm,), in_specs=in_specs, out_specs=out_specs, out_shape=out_shape,
        compiler_params=_cparams(("parallel",), vmem), name="rmsnorm")(*args)
    return outs


def _w_spec(w, layer_idx, k, tn, kblock=0):
    lead = tuple(layer_idx)
    nlead = len(lead)
    assert w.ndim == nlead + 2
    return pl.BlockSpec((None,) * nlead + (k, tn), lambda j, i: lead + (kblock, j))


def _mm_nt_body(x_ref, w_ref, o_ref, wb_sc):
    @pl.when(_first_inner_step())
    def _():
        wb_sc[...] = w_ref[0].astype(BF16)

    o_ref[...] = _dot_nt(x_ref[...], wb_sc[...]).astype(o_ref.dtype)


def mm_nt(x, wt, layer, row0, n, out_dtype, tm_pref=1024, tn_pref=512):
    m, k = x.shape
    tm, tn = _tile(m, tm_pref, SUBLANE), _tile(n, tn_pref)
    assert row0 % SUBLANE == 0
    w_spec = pl.BlockSpec((pl.Element(1), pl.Element(tn), pl.Element(k)), lambda j, i: (layer, pl.multiple_of(row0 + j * tn, SUBLANE), 0))
    vmem = 2 * (tm * k * 2 + k * tn * 4 + tm * tn * 4) + k * tn * 2 + tm * tn * 4 + (4 << 20)
    return pl.pallas_call(
        _mm_nt_body, grid=(n // tn, m // tm),
        in_specs=[pl.BlockSpec((tm, k), lambda j, i: (i, 0)), w_spec],
        out_specs=pl.BlockSpec((tm, tn), lambda j, i: (i, j)),
        out_shape=jax.ShapeDtypeStruct((m, n), out_dtype),
        scratch_shapes=[pltpu.VMEM((tn, k), BF16)],
        compiler_params=_cparams(("parallel", "arbitrary"), vmem), name="mm_in")(x, wt)


def _mm_swiglu_body(x_ref, wg_ref, wu_ref, o_ref, wg_sc, wu_sc):
    @pl.when(_first_inner_step())
    def _():
        wg_sc[...] = wg_ref[...].astype(BF16)
        wu_sc[...] = wu_ref[...].astype(BF16)

    x = x_ref[...]
    g = _dot(x, wg_sc[...])
    u = _dot(x, wu_sc[...])
    o_ref[...] = (g * _sigmoid(g) * u).astype(o_ref.dtype)


def mm_swiglu(x, wg, wu, layer_idx, tm_pref=1024, tn_pref=256):
    m, k = x.shape
    n = wg.shape[-1]
    tm, tn = _tile(m, tm_pref, SUBLANE), _tile(n, tn_pref)
    vmem = 2 * (tm * k * 2 + 2 * k * tn * 4 + tm * tn * 2) + 2 * k * tn * 2 + 3 * tm * tn * 4 + (4 << 20)
    return pl.pallas_call(
        _mm_swiglu_body, grid=(n // tn, m // tm),
        in_specs=[pl.BlockSpec((tm, k), lambda j, i: (i, 0)),
                  _w_spec(wg, layer_idx, k, tn), _w_spec(wu, layer_idx, k, tn)],
        out_specs=pl.BlockSpec((tm, tn), lambda j, i: (i, j)),
        out_shape=jax.ShapeDtypeStruct((m, n), BF16),
        scratch_shapes=[pltpu.VMEM((k, tn), BF16), pltpu.VMEM((k, tn), BF16)],
        compiler_params=_cparams(("parallel", "arbitrary"), vmem), name="mm_swiglu")(x, wg, wu)


def _mm2_res_body(x1_ref, x2_ref, w1_ref, w2_ref, r_ref, o_ref, w1_sc, w2_sc):
    @pl.when(_first_inner_step())
    def _():
        w1_sc[...] = w1_ref[...].astype(BF16)
        w2_sc[...] = w2_ref[...].astype(BF16)

    acc = _dot(x1_ref[...], w1_sc[...]) + _dot(x2_ref[...], w2_sc[...])
    o_ref[...] = r_ref[...] + acc


def mm2_res(x1, x2, w, layer_idx, res, tm_pref=1024, tn_pref=512):
    m, k1 = x1.shape
    k2 = x2.shape[1]
    assert k1 == k2
    n = w.shape[-1]
    tm, tn = _tile(m, tm_pref, SUBLANE), _tile(n, tn_pref)
    vmem = 2 * (2 * tm * k1 * 2 + 2 * k1 * tn * 4 + 2 * tm * tn * 4) + 2 * k1 * tn * 2 + tm * tn * 4 + (4 << 20)
    return pl.pallas_call(
        _mm2_res_body, grid=(n // tn, m // tm),
        in_specs=[pl.BlockSpec((tm, k1), lambda j, i: (i, 0)), pl.BlockSpec((tm, k2), lambda j, i: (i, 0)),
                  _w_spec(w, layer_idx, k1, tn, 0), _w_spec(w, layer_idx, k2, tn, 1),
                  pl.BlockSpec((tm, tn), lambda j, i: (i, j))],
        out_specs=pl.BlockSpec((tm, tn), lambda j, i: (i, j)),
        out_shape=jax.ShapeDtypeStruct((m, n), F32),
        scratch_shapes=[pltpu.VMEM((k1, tn), BF16), pltpu.VMEM((k2, tn), BF16)],
        compiler_params=_cparams(("parallel", "arbitrary"), vmem), name="mm_wo")(x1, x2, w, w, res)


def _mm_ple_body(x_ref, w_ref, r_ref, e_ref, o_ref, w_sc):
    @pl.when(_first_inner_step())
    def _():
        w_sc[...] = w_ref[...].astype(BF16)

    acc = _dot(x_ref[...], w_sc[...])
    o_ref[...] = r_ref[...] + _sigmoid(acc) * e_ref[...].astype(F32)


def mm_ple(x, w, layer_idx, res, e, tm_pref=1024, tn_pref=512):
    m, k = x.shape
    n = w.shape[-1]
    tm, tn = _tile(m, tm_pref, SUBLANE), _tile(n, tn_pref)
    vmem = 2 * (tm * k * 2 + k * tn * 4 + 3 * tm * tn * 4) + k * tn * 2 + 2 * tm * tn * 4 + (4 << 20)
    return pl.pallas_call(
        _mm_ple_body, grid=(n // tn, m // tm),
        in_specs=[pl.BlockSpec((tm, k), lambda j, i: (i, 0)), _w_spec(w, layer_idx, k, tn),
                  pl.BlockSpec((tm, tn), lambda j, i: (i, j)), pl.BlockSpec((tm, tn), lambda j, i: (i, j))],
        out_specs=pl.BlockSpec((tm, tn), lambda j, i: (i, j)),
        out_shape=jax.ShapeDtypeStruct((m, n), F32),
        scratch_shapes=[pltpu.VMEM((k, tn), BF16)],
        compiler_params=_cparams(("parallel", "arbitrary"), vmem), name="mm_ple")(x, w, res, e)


def _mm_res_body(x_ref, w_ref, r_ref, o_ref, w_sc):
    @pl.when(_first_inner_step())
    def _():
        w_sc[...] = w_ref[...].astype(BF16)

    o_ref[...] = r_ref[...] + _dot(x_ref[...], w_sc[...])


def mm_down(x, w, layer_idx, res, tm_pref=512, tn_pref=512, tk_pref=5632):
    m, k = x.shape
    n = w.shape[-1]
    tm, tn, tk = _tile(m, tm_pref, SUBLANE), _tile(n, tn_pref), _tile(k, tk_pref)
    vmem = 2 * (tm * tk * 2 + tk * tn * 4 + 2 * tm * tn * 4) + tk * tn * 2 + tm * tn * 4 + (4 << 20)
    for kb in range(k // tk):
        res = pl.pallas_call(
            _mm_res_body, grid=(n // tn, m // tm),
            in_specs=[pl.BlockSpec((tm, tk), lambda j, i, kb=kb: (i, kb)), _w_spec(w, layer_idx, tk, tn, kb),
                      pl.BlockSpec((tm, tn), lambda j, i: (i, j))],
            out_specs=pl.BlockSpec((tm, tn), lambda j, i: (i, j)),
            out_shape=jax.ShapeDtypeStruct((m, n), F32),
            scratch_shapes=[pltpu.VMEM((tk, tn), BF16)],
            compiler_params=_cparams(("parallel", "arbitrary"), vmem), name="mm_down")(x, w, res)
    return res


def _ple_embed_body(p_ref, w_ref, g_ref, o_ref):
    e = _dot(p_ref[...].astype(BF16), w_ref[...].astype(BF16))
    o_ref[...] = _rms(e, g_ref[...]).astype(o_ref.dtype)


def ple_embed(p, w_pe, layer, g):
    t, kp = p.shape
    d = w_pe.shape[-1]
    tm = _tile(t, 512, SUBLANE)
    vmem = 2 * (tm * kp * 4 + kp * d * 4 + tm * d * 2) + 2 * tm * d * 4 + kp * d * 2 + (4 << 20)
    return pl.pallas_call(
        _ple_embed_body, grid=(t // tm,),
        in_specs=[pl.BlockSpec((tm, kp), lambda i: (i, 0)),
                  pl.BlockSpec((None, kp, d), lambda i: (layer, 0, 0)),
                  pl.BlockSpec((1, d), lambda i: (0, 0))],
        out_specs=pl.BlockSpec((tm, d), lambda i: (i, 0)),
        out_shape=jax.ShapeDtypeStruct((t, d), BF16),
        compiler_params=_cparams(("parallel",), vmem), name="ple_embed")(p, w_pe, g.reshape(1, d))


def _rotate(xg, tab):
    return xg * tab[:, 0:LANE] + pltpu.roll(xg, ROPE // 2, 1) * tab[:, LANE:2 * LANE]


def _qkv_post_body(z_ref, tab_ref, wuq_ref, wuk_ref, gqa_ref, gq_ref, gkva_ref, gk_ref, *outs,
                   nh, qlr, sample, q_scale):
    if sample:
        q_ref, ckv_ref, kr_ref = outs
    else:
        q_ref, ckv_ref, kr_ref, kk_ref = outs
    tm = z_ref.shape[0]
    tab = tab_ref[...]
    lane = lax.broadcasted_iota(jnp.int32, (tm, LANE), 1)
    lo = lane < ROPE

    qa = _rms(z_ref[:, 0:qlr], gqa_ref[...]).astype(BF16)
    q = _dot(qa, wuq_ref[...])
    gq = gq_ref[...]
    for h in range(nh):
        qn = q[:, h * QH:h * QH + NOPE]
        qr = q[:, h * QH + NOPE:(h + 1) * QH]
        ssq = jnp.sum(qn * qn, axis=-1, keepdims=True) + jnp.sum(jnp.where(lo, qr * qr, 0.0), axis=-1, keepdims=True)
        sc = lax.rsqrt(ssq * (1.0 / (NOPE + ROPE)) + RMS_EPS)
        qn = (qn * sc * gq[:, 0:NOPE]).astype(BF16)
        qlat = _dot(qn, wuk_ref[h]) * q_scale
        rot = jnp.where(lo, _rotate(qr * sc * gq[:, NOPE:QH], tab), 0.0) * q_scale
        if sample:
            q_ref[:, h, :, 0:KVR] = qlat.reshape(tm // SUBLANE, SUBLANE, KVR)
            q_ref[:, h, :, KVR:QW] = rot.reshape(tm // SUBLANE, SUBLANE, LANE)
        else:
            q_ref[h, :, 0:KVR] = qlat.astype(q_ref.dtype)
            q_ref[h, :, KVR:QW] = rot.astype(q_ref.dtype)

    c = _rms(z_ref[:, qlr:qlr + KVR], gkva_ref[...])
    ckv_ref[...] = c
    k1 = z_ref[:, qlr + KVR:qlr + KVR + LANE]
    k2 = jnp.where(lo, k1, pltpu.roll(k1, ROPE, 1))
    ssq = jnp.sum(jnp.where(lo, k2 * k2, 0.0), axis=-1, keepdims=True)
    kr = _rotate(k2 * lax.rsqrt(ssq * (1.0 / ROPE) + RMS_EPS) * gk_ref[...], tab)
    kr_ref[...] = kr[:, 0:ROPE]
    if not sample:
        kk_ref[:, 0:KVR] = c.astype(BF16)
        kk_ref[:, KVR:QW] = jnp.where(lo, kr, 0.0).astype(BF16)


def qkv_post(z1, row0, rows, tab, tab_period, wuq, wuk_t, gqa, gq, gkva, gk, nh, sample, q_scale):
    qlr = gqa.shape[-1]
    tm = _tile(rows, 256, 16) if not sample else _tile(rows, 128, SUBLANE)
    assert row0 % tm == 0 and tab_period % tm == 0
    rb0 = row0 // tm
    tb = tab_period // tm
    zw = z1.shape[1]
    in_specs = [pl.BlockSpec((tm, zw), lambda i: (rb0 + i, 0)),
                pl.BlockSpec((tm, 2 * LANE), lambda i: (i % tb, 0)),
                pl.BlockSpec(wuq.shape, lambda i: (0, 0)),
                pl.BlockSpec(wuk_t.shape, lambda i: (0, 0, 0)),
                pl.BlockSpec((1, qlr), lambda i: (0, 0)),
                pl.BlockSpec((1, QH), lambda i: (0, 0)),
                pl.BlockSpec((1, KVR), lambda i: (0, 0)),
                pl.BlockSpec((1, LANE), lambda i: (0, 0))]
    if sample:
        q_shape = jax.ShapeDtypeStruct((rows // SUBLANE, nh, SUBLANE, QW), F32)
        q_spec = pl.BlockSpec((tm // SUBLANE, nh, SUBLANE, QW), lambda i: (i, 0, 0, 0))
    else:
        q_shape = jax.ShapeDtypeStruct((nh, rows, QW), BF16)
        q_spec = pl.BlockSpec((nh, tm, QW), lambda i: (0, i, 0))
    out_shape = [q_shape, jax.ShapeDtypeStruct((rows, KVR), F32), jax.ShapeDtypeStruct((rows, ROPE), F32)]
    out_specs = [q_spec, pl.BlockSpec((tm, KVR), lambda i: (i, 0)), pl.BlockSpec((tm, ROPE), lambda i: (i, 0))]
    if not sample:
        out_shape.append(jax.ShapeDtypeStruct((rows, QW), BF16))
        out_specs.append(pl.BlockSpec((tm, QW), lambda i: (i, 0)))
    vmem = 2 * (tm * zw * 4 + wuq.size * 2 + wuk_t.size * 2 + nh * tm * QW * 4) + 3 * tm * nh * QH * 4 + (8 << 20)
    return pl.pallas_call(
        functools.partial(_qkv_post_body, nh=nh, qlr=qlr, sample=sample, q_scale=q_scale),
        grid=(rows // tm,), in_specs=in_specs, out_specs=out_specs, out_shape=out_shape,
        compiler_params=_cparams(("parallel",), vmem), name="qkv_post_s" if sample else "qkv_post_p",
    )(z1, tab, wuq, wuk_t, gqa.reshape(1, qlr), gq, gkva.reshape(1, KVR), gk)


def _attn_p_body(q_ref, kk_ref, wuv_ref, gao_ref, o_ref, m_sc, l_sc, acc_sc, *, nh, tq, tk):
    qi = pl.program_id(1)
    ki = pl.program_id(2)
    last_k = (qi * tq + tq - 1) // tk
    rows = nh * tq

    @pl.when(ki == 0)
    def _():
        m_sc[...] = jnp.full(m_sc.shape, -jnp.inf, F32)
        l_sc[...] = jnp.zeros(l_sc.shape, F32)
        acc_sc[...] = jnp.zeros(acc_sc.shape, F32)

    def update(masked):
        q = q_ref[...].reshape(rows, QW)
        k = kk_ref[...]
        s = _dot_nt(q, k)
        if masked:
            qpos = (lax.broadcasted_iota(jnp.int32, (rows, tk), 0) & (tq - 1)) + qi * tq
            kpos = lax.broadcasted_iota(jnp.int32, (rows, tk), 1) + ki * tk
            s = jnp.where(kpos <= qpos, s, -jnp.inf)
        m_prev = m_sc[...]
        m_new = jnp.maximum(m_prev, jnp.max(s, axis=-1, keepdims=True))
        alpha = jnp.exp2(m_prev - m_new)
        p = jnp.exp2(s - m_new)
        l_sc[...] = alpha * l_sc[...] + jnp.sum(p, axis=-1, keepdims=True)
        acc_sc[...] = alpha * acc_sc[...] + _dot(p.astype(BF16), k[:, 0:KVR])
        m_sc[...] = m_new

    @pl.when(ki < last_k)
    def _():
        update(False)

    @pl.when(ki == last_k)
    def _():
        update(True)
        inv = 1.0 / l_sc[...]
        for h in range(nh):
            sl = slice(h * tq, (h + 1) * tq)
            o = (acc_sc[sl, :] * inv[sl, :]).astype(BF16)
            v = _dot(o, wuv_ref[h])
            o_ref[:, h * VH:(h + 1) * VH] = _rms(v, gao_ref[h]).astype(o_ref.dtype)


def attn_prompt(q, kk, wuv_t, g_ao, b, s, nh):
    tq = _tile(s, 128, 16)
    tk = _tile(s, 512, 16)
    assert tq & (tq - 1) == 0 and tk % tq == 0
    nq, nk = s // tq, s // tk
    rows = nh * tq

    def kk_map(bi, qi, ki):
        return (bi * nk + jnp.minimum(ki, (qi * tq + tq - 1) // tk), 0)

    vmem = 2 * (rows * QW * 2 + tk * QW * 2 + tq * nh * VH * 2 + wuv_t.size * 2) + rows * (KVR + 2 * LANE) * 4 \
        + 4 * rows * tk * 4 + (8 << 20)
    return pl.pallas_call(
        functools.partial(_attn_p_body, nh=nh, tq=tq, tk=tk),
        grid=(b, nq, nk),
        in_specs=[pl.BlockSpec((nh, tq, QW), lambda bi, qi, ki: (0, bi * nq + qi, 0)),
                  pl.BlockSpec((tk, QW), kk_map),
                  pl.BlockSpec(wuv_t.shape, lambda bi, qi, ki: (0, 0, 0)),
                  pl.BlockSpec(g_ao.shape, lambda bi, qi, ki: (0, 0, 0))],
        out_specs=pl.BlockSpec((tq, nh * VH), lambda bi, qi, ki: (bi * nq + qi, 0)),
        out_shape=jax.ShapeDtypeStruct((b * s, nh * VH), BF16),
        scratch_shapes=[pltpu.VMEM((rows, 1), F32), pltpu.VMEM((rows, 1), F32), pltpu.VMEM((rows, KVR), F32)],
        compiler_params=_cparams(("parallel", "parallel", "arbitrary"), vmem), name="attn_prompt",
    )(q, kk, wuv_t, g_ao)


def _attn_s_body(pt_ref, q_ref, cn_ref, kn_ref, ckv_hbm, krt_hbm, o_ref, cbuf, kbuf, csem, ksem,
                 *, nh, ds, n_pages, page, layer, n_chunk):
    b = pl.program_id(0)
    nb = pl.num_programs(0)
    slot = lax.rem(b, 2)
    nxt = jnp.minimum(b + 1, nb - 1)
    rows = nh * ds

    def page_copies(seq, sl, i):
        pg = pt_ref[seq * n_pages + i]
        return (pltpu.make_async_copy(ckv_hbm.at[layer, pg], cbuf.at[sl, pl.ds(i * page, page), :], csem.at[sl]),
                pltpu.make_async_copy(krt_hbm.at[layer, pg], kbuf.at[sl, :, pl.ds(i * page, page)], ksem.at[sl]))

    def start_all(seq, sl):
        for i in range(n_pages):
            for cp in page_copies(seq, sl, i):
                cp.start()

    def wait_all(seq, sl):
        for i in range(n_pages):
            for cp in page_copies(seq, sl, i):
                cp.wait()

    @pl.when(b == 0)
    def _():
        start_all(0, 0)

    wait_all(b, slot)
    start_all(nxt, 1 - slot)

    q = q_ref[...].reshape(rows, QW).astype(BF16)
    q_lat = q[:, 0:KVR]
    q_rope = q[:, KVR:KVR + ROPE]
    pad = LANE - ds
    c0 = jnp.concatenate([cn_ref[...], jnp.zeros((pad, KVR), F32)], axis=0).astype(BF16)
    k0 = jnp.concatenate([kn_ref[...], jnp.zeros((pad, ROPE), F32)], axis=0).astype(BF16)
    s0 = _dot_nt(q_lat, c0) + _dot_nt(q_rope, k0)
    tpos = lax.broadcasted_iota(jnp.int32, (rows, LANE), 0) & (ds - 1)
    kpos = lax.broadcasted_iota(jnp.int32, (rows, LANE), 1)
    s0 = jnp.where(kpos <= tpos, s0, -jnp.inf)

    ck = n_pages * page // n_chunk
    cs, ss = [], []
    for j in range(n_chunk):
        c = cbuf[slot, j * ck:(j + 1) * ck, :].astype(BF16)
        kt = kbuf[slot, :, j * ck:(j + 1) * ck].astype(BF16)
        cs.append(c)
        ss.append(_dot_nt(q_lat, c) + _dot(q_rope, kt))
    m = jnp.max(s0, axis=-1, keepdims=True)
    for s in ss:
        m = jnp.maximum(m, jnp.max(s, axis=-1, keepdims=True))
    p0 = jnp.exp2(s0 - m)
    l = jnp.sum(p0, axis=-1, keepdims=True)
    acc = _dot(p0.astype(BF16), c0)
    for c, s in zip(cs, ss):
        p = jnp.exp2(s - m)
        l = l + jnp.sum(p, axis=-1, keepdims=True)
        acc = acc + _dot(p.astype(BF16), c)
    o_ref[...] = (acc / l).reshape(nh, ds, KVR)

    @pl.when(b == nb - 1)
    def _():
        wait_all(nxt, 1 - slot)


def attn_sample(q, c_new, k_new, cache_ckv, cache_krope_t, page_table, layer, nh):
    db, _, ds, _ = q.shape
    n_pages = page_table.shape[1]
    page = cache_ckv.shape[2]
    assert ds == SUBLANE and ds & (ds - 1) == 0
    keys = n_pages * page
    n_chunk = next(c for c in (4, 2, 1) if n_pages % c == 0)
    rows = nh * ds
    grid_spec = pltpu.PrefetchScalarGridSpec(
        num_scalar_prefetch=1, grid=(db,),
        in_specs=[pl.BlockSpec((None, nh, ds, QW), lambda b, pt_ref: (b, 0, 0, 0)),
                  pl.BlockSpec((ds, KVR), lambda b, pt_ref: (b, 0)),
                  pl.BlockSpec((ds, ROPE), lambda b, pt_ref: (b, 0)),
                  pl.BlockSpec(memory_space=pl.ANY), pl.BlockSpec(memory_space=pl.ANY)],
        out_specs=pl.BlockSpec((None, nh, ds, KVR), lambda b, pt_ref: (b, 0, 0, 0)),
        scratch_shapes=[pltpu.VMEM((2, keys, KVR), F32), pltpu.VMEM((2, ROPE, keys), F32),
                        pltpu.SemaphoreType.DMA((2,)), pltpu.SemaphoreType.DMA((2,))])
    vmem = 2 * keys * (KVR + ROPE) * 4 + 2 * keys * (KVR + ROPE) * 2 + 3 * rows * keys * 4 + (8 << 20)
    return pl.pallas_call(
        functools.partial(_attn_s_body, nh=nh, ds=ds, n_pages=n_pages, page=page, layer=layer, n_chunk=n_chunk),
        grid_spec=grid_spec, out_shape=jax.ShapeDtypeStruct((db, nh, ds, KVR), F32),
        compiler_params=_cparams(("arbitrary",), vmem), name="attn_sample",
    )(page_table.reshape(-1), q, c_new, k_new, cache_ckv, cache_krope_t)


def _ouv_body(o_ref, w_ref, g_ref, out_ref):
    bx, by, _ = o_ref.shape
    o = o_ref[...].reshape(bx * by, KVR).astype(BF16)
    v = _dot(o, w_ref[...])
    out_ref[...] = _rms(v, g_ref[...]).astype(out_ref.dtype)


def ouv(o_lat, wuv_t, g_ao, bx, by):
    x, nh, y, _ = o_lat.shape
    tm = bx * by
    nyb = y // by
    vmem = 2 * (tm * KVR * 4 + KVR * VH * 2 + tm * VH * 2) + 4 * tm * KVR * 4 + (4 << 20)
    return pl.pallas_call(
        _ouv_body, grid=(x // bx, nyb, nh),
        in_specs=[pl.BlockSpec((bx, None, by, KVR), lambda xi, yi, h: (xi, h, yi, 0)),
                  pl.BlockSpec((None, KVR, VH), lambda xi, yi, h: (h, 0, 0)),
                  pl.BlockSpec((None, 1, VH), lambda xi, yi, h: (h, 0, 0))],
        out_specs=pl.BlockSpec((tm, VH), lambda xi, yi, h: (xi * nyb + yi, h)),
        out_shape=jax.ShapeDtypeStruct((x * y, nh * VH), BF16),
        compiler_params=_cparams(("parallel", "parallel", "arbitrary"), vmem), name="ouv")(o_lat, wuv_t, g_ao)


def _conv_core(gb, gc, xc, p0, p1, period, w_ref, g_ref, y_ref, groups):
    tm, c = gb.shape
    u = gc.astype(F32) * xc.astype(F32)
    t = lax.broadcasted_iota(jnp.int32, (tm, c), 0) & (period - 1)
    um1 = jnp.where(t == 0, p1, pltpu.roll(u, 1, 0))
    um2 = jnp.where(t == 0, p0, jnp.where(t == 1, p1, pltpu.roll(u, 2, 0)))
    w = w_ref[...]
    y = um2 * w[0:1, :] + um1 * w[1:2, :] + u * w[2:3, :]
    y = gb.astype(F32) * y
    gw = c // groups
    g = g_ref[...]
    for i in range(groups):
        sl = slice(i * gw, (i + 1) * gw)
        y_ref[:, sl] = _rms(y[:, sl], g[:, sl]).astype(y_ref.dtype)
    return u


def _conv_p_body(gb_ref, gc_ref, xc_ref, w_ref, g_ref, y_ref, st_ref, prev_sc, *, groups):
    ts = gb_ref.shape[0]

    @pl.when(pl.program_id(1) == 0)
    def _():
        prev_sc[...] = jnp.zeros(prev_sc.shape, F32)

    u = _conv_core(gb_ref[...], gc_ref[...], xc_ref[...], prev_sc[0:1, :], prev_sc[1:2, :], ts,
                   w_ref, g_ref, y_ref, groups)
    tail = u[ts - 2:ts, :]
    prev_sc[0:2, :] = tail
    st_ref[...] = tail


def conv_prompt(z2, b, s, conv_w, g_co, groups):
    c = conv_w.shape[-1]
    ts = _tile(s, 256, 16)
    assert ts & (ts - 1) == 0 and conv_w.shape[0] == 3
    ns = s // ts

    def col(j):
        return pl.BlockSpec((ts, c), lambda bi, si: (bi * ns + si, j))

    vmem = 2 * (3 * ts * c * 2 + ts * c * 2) + 8 * ts * c * 4 + (4 << 20)
    return pl.pallas_call(
        functools.partial(_conv_p_body, groups=groups), grid=(b, ns),
        in_specs=[col(0), col(1), col(2), pl.BlockSpec((3, c), lambda bi, si: (0, 0)),
                  pl.BlockSpec((1, c), lambda bi, si: (0, 0))],
        out_specs=[pl.BlockSpec((ts, c), lambda bi, si: (bi * ns + si, 0)),
                   pl.BlockSpec((None, 2, c), lambda bi, si: (bi, 0, 0))],
        out_shape=[jax.ShapeDtypeStruct((b * s, c), BF16), jax.ShapeDtypeStruct((b, 2, c), F32)],
        scratch_shapes=[pltpu.VMEM((SUBLANE, c), F32)],
        compiler_params=_cparams(("parallel", "arbitrary"), vmem), name="conv_prompt",
    )(z2, z2, z2, conv_w, g_co.reshape(1, c))


def _conv_s_body(gb_ref, gc_ref, xc_ref, st_ref, w_ref, g_ref, y_ref, new_ref, *, groups, ds):
    tm, c = gb_ref.shape
    bb = tm // ds
    st = st_ref[...]
    p0 = jnp.broadcast_to(st[:, 0:1, :], (bb, ds, c)).reshape(tm, c)
    p1 = jnp.broadcast_to(st[:, 1:2, :], (bb, ds, c)).reshape(tm, c)
    u = _conv_core(gb_ref[...], gc_ref[...], xc_ref[...], p0, p1, ds, w_ref, g_ref, y_ref, groups)
    new_ref[...] = u.reshape(bb, ds, c)[:, ds - 2:ds, :]


def conv_sample(z2, row0, db, ds, state, conv_w, g_co, groups):
    c = conv_w.shape[-1]
    assert ds == SUBLANE
    bb = _tile(db, 16, 1)
    tm = bb * ds
    assert row0 % tm == 0
    rb0 = row0 // tm

    def col(j):
        return pl.BlockSpec((tm, c), lambda i: (rb0 + i, j))

    vmem = 2 * (3 * tm * c * 2 + tm * c * 2 + 2 * bb * SUBLANE * c * 4) + 10 * tm * c * 4 + (4 << 20)
    return pl.pallas_call(
        functools.partial(_conv_s_body, groups=groups, ds=ds), grid=(db // bb,),
        in_specs=[col(0), col(1), col(2), pl.BlockSpec((bb, 2, c), lambda i: (i, 0, 0)),
                  pl.BlockSpec((3, c), lambda i: (0, 0)), pl.BlockSpec((1, c), lambda i: (0, 0))],
        out_specs=[pl.BlockSpec((tm, c), lambda i: (i, 0)), pl.BlockSpec((bb, 2, c), lambda i: (i, 0, 0))],
        out_shape=[jax.ShapeDtypeStruct((db * ds, c), BF16), jax.ShapeDtypeStruct((db, 2, c), F32)],
        compiler_params=_cparams(("parallel",), vmem), name="conv_sample",
    )(z2, z2, z2, state, conv_w, g_co.reshape(1, c))


def _gather_body(idx_ref, valid_ref, src_ref, o_ref, buf, sem, *, rows):
    i = pl.program_id(0)

    def row_copy(r, src_row):
        return pltpu.make_async_copy(src_ref.at[pl.ds(src_row, 1)], buf.at[pl.ds(r, 1)], sem)

    @pl.when(valid_ref[i] != 0)
    def _():
        def issue(r, carry):
            row_copy(r, idx_ref[i * rows + r]).start()
            return carry

        lax.fori_loop(0, rows, issue, 0)

        def wait(r, carry):
            row_copy(r, 0).wait()
            return carry

        lax.fori_loop(0, rows, wait, 0)
        o_ref[...] = buf[...].astype(o_ref.dtype)

    @pl.when(valid_ref[i] == 0)
    def _():
        o_ref[...] = jnp.zeros(o_ref.shape, o_ref.dtype)


def gather_rows(src, idx, tile_valid, rows):
    p = idx.shape[0]
    d = src.shape[1]
    grid_spec = pltpu.PrefetchScalarGridSpec(
        num_scalar_prefetch=2, grid=(p // rows,),
        in_specs=[pl.BlockSpec(memory_space=pl.ANY)],
        out_specs=pl.BlockSpec((rows, d), lambda i, idx_ref, v_ref: (i, 0)),
        scratch_shapes=[pltpu.VMEM((rows, d), src.dtype), pltpu.SemaphoreType.DMA(())])
    vmem = rows * d * 4 + 2 * rows * d * 2 + rows * d * 4 + (4 << 20)
    return pl.pallas_call(
        functools.partial(_gather_body, rows=rows), grid_spec=grid_spec,
        out_shape=jax.ShapeDtypeStruct((p, d), BF16),
        compiler_params=_cparams(("arbitrary",), vmem), name="moe_gather")(idx, tile_valid, src)


def _combine_body(p0_ref, p1_ref, ys_ref, h_ref, o_ref, b0, b1, sem, *, rows):
    i = pl.program_id(0)

    def row_copy(r, src_row, buf):
        return pltpu.make_async_copy(ys_ref.at[pl.ds(src_row, 1)], buf.at[pl.ds(r, 1)], sem)

    def issue(r, carry):
        row_copy(r, p0_ref[i * rows + r], b0).start()
        row_copy(r, p1_ref[i * rows + r], b1).start()
        return carry

    lax.fori_loop(0, rows, issue, 0)

    def wait(r, carry):
        row_copy(r, 0, b0).wait()
        row_copy(r, 0, b1).wait()
        return carry

    lax.fori_loop(0, rows, wait, 0)
    o_ref[...] = h_ref[...] + b0[...] + b1[...]


def moe_combine(ys, pos0, pos1, h, rows=256):
    t, d = h.shape
    rows = _tile(t, rows, SUBLANE)
    grid_spec = pltpu.PrefetchScalarGridSpec(
        num_scalar_prefetch=2, grid=(t // rows,),
        in_specs=[pl.BlockSpec(memory_space=pl.ANY), pl.BlockSpec((rows, d), lambda i, a, b: (i, 0))],
        out_specs=pl.BlockSpec((rows, d), lambda i, a, b: (i, 0)),
        scratch_shapes=[pltpu.VMEM((rows, d), F32), pltpu.VMEM((rows, d), F32), pltpu.SemaphoreType.DMA(())])
    vmem = 6 * rows * d * 4 + rows * d * 4 + (4 << 20)
    return pl.pallas_call(
        functools.partial(_combine_body, rows=rows), grid_spec=grid_spec,
        out_shape=jax.ShapeDtypeStruct((t, d), F32),
        compiler_params=_cparams(("arbitrary",), vmem), name="moe_combine")(pos0, pos1, ys, h)


def _new_expert(te_ref):
    i = pl.program_id(1)
    return jnp.logical_or(i == 0, te_ref[i] != te_ref[jnp.maximum(i - 1, 0)])


def _by_row_count(nrows, o_ref, fn):
    tm = o_ref.shape[0]
    half = tm // 2

    @pl.when(nrows > half)
    def _():
        o_ref[...] = fn(slice(0, tm))

    @pl.when(jnp.logical_and(nrows > 0, nrows <= half))
    def _():
        o_ref[0:half, :] = fn(slice(0, half))
        o_ref[half:tm, :] = jnp.zeros((tm - half, o_ref.shape[1]), o_ref.dtype)

    @pl.when(nrows == 0)
    def _():
        o_ref[...] = jnp.zeros(o_ref.shape, o_ref.dtype)


def _moe_up_body(te_ref, tb_ref, tv_ref, x_ref, wg_ref, wu_ref, o_ref, wg_sc, wu_sc):
    valid = tv_ref[pl.program_id(1)] != 0

    @pl.when(jnp.logical_and(valid, _new_expert(te_ref)))
    def _():
        wg_sc[...] = wg_ref[...].astype(BF16)
        wu_sc[...] = wu_ref[...].astype(BF16)

    def swiglu(x):
        g = _dot(x, wg_sc[...])
        u = _dot(x, wu_sc[...])
        return (g * _sigmoid(g) * u).astype(o_ref.dtype)

    _by_row_count(tv_ref[pl.program_id(1)], o_ref, lambda sl: swiglu(x_ref[sl, :]))


def moe_up_call(xs, wg, wu, j, tile_expert, tile_block, tile_valid, tm, tf_pref=512):
    p, k = xs.shape
    f = wg.shape[-1]
    tf = _tile(f, tf_pref)

    def w_map(fi, i, te, tb, tv):
        return (j, te[i], 0, fi)

    grid_spec = pltpu.PrefetchScalarGridSpec(
        num_scalar_prefetch=3, grid=(f // tf, p // tm),
        in_specs=[pl.BlockSpec((tm, k), lambda fi, i, te, tb, tv: (tb[i], 0)),
                  pl.BlockSpec((None, None, k, tf), w_map), pl.BlockSpec((None, None, k, tf), w_map)],
        out_specs=pl.BlockSpec((tm, tf), lambda fi, i, te, tb, tv: (i, fi)),
        scratch_shapes=[pltpu.VMEM((k, tf), BF16), pltpu.VMEM((k, tf), BF16)])
    vmem = 2 * (tm * k * 2 + 2 * k * tf * 4 + tm * tf * 2) + 2 * k * tf * 2 + 3 * tm * tf * 4 + (4 << 20)
    return pl.pallas_call(
        _moe_up_body, grid_spec=grid_spec, out_shape=jax.ShapeDtypeStruct((p, f), BF16),
        compiler_params=_cparams(("parallel", "arbitrary"), vmem), name="moe_up",
    )(tile_expert, tile_block, tile_valid, xs, wg, wu)


def _moe_down_body(te_ref, tb_ref, tv_ref, a_ref, w_ref, g_ref, o_ref, w_sc):
    valid = tv_ref[pl.program_id(1)] != 0

    @pl.when(jnp.logical_and(valid, _new_expert(te_ref)))
    def _():
        w_sc[...] = w_ref[...].astype(BF16)

    _by_row_count(tv_ref[pl.program_id(1)], o_ref, lambda sl: _dot(a_ref[sl, :], w_sc[...]) * g_ref[sl, :])


def moe_down_call(act, wd, j, row_gate, tile_expert, tile_block, tile_valid, tm, tn_pref=512):
    p, k = act.shape
    n = wd.shape[-1]
    tn = _tile(n, tn_pref)
    grid_spec = pltpu.PrefetchScalarGridSpec(
        num_scalar_prefetch=3, grid=(n // tn, p // tm),
        in_specs=[pl.BlockSpec((tm, k), lambda ni, i, te, tb, tv: (tb[i], 0)),
                  pl.BlockSpec((None, None, k, tn), lambda ni, i, te, tb, tv: (j, te[i], 0, ni)),
                  pl.BlockSpec((tm, 1), lambda ni, i, te, tb, tv: (tb[i], 0))],
        out_specs=pl.BlockSpec((tm, tn), lambda ni, i, te, tb, tv: (i, ni)),
        scratch_shapes=[pltpu.VMEM((k, tn), BF16)])
    vmem = 2 * (tm * k * 2 + k * tn * 4 + tm * tn * 4 + tm * LANE * 4) + k * tn * 2 + tm * tn * 4 + (4 << 20)
    return pl.pallas_call(
        _moe_down_body, grid_spec=grid_spec, out_shape=jax.ShapeDtypeStruct((p, n), F32),
        compiler_params=_cparams(("parallel", "arbitrary"), vmem), name="moe_down",
    )(tile_expert, tile_block, tile_valid, act, wd, row_gate)


def moe_route(logits, n_exp, tm):
    t = logits.shape[0]
    top_val, top_idx = lax.top_k(logits[:, :n_exp], TOP_K)
    wts = jax.nn.softmax(top_val, axis=-1)
    e_flat = top_idx.T.reshape(-1).astype(jnp.int32)
    g_flat = wts.T.reshape(-1)
    tok = jnp.tile(jnp.arange(t, dtype=jnp.int32), TOP_K)
    counts = jnp.sum(jax.nn.one_hot(e_flat, n_exp, dtype=jnp.int32), axis=0)
    padded = ((counts + tm - 1) // tm) * tm
    pad_end = jnp.cumsum(padded)
    pad_start = pad_end - padded
    start = jnp.cumsum(counts) - counts
    order = jnp.argsort(e_flat, stable=True)
    e_sorted = e_flat[order]
    dest_sorted = pad_start[e_sorted] + jnp.arange(TOP_K * t, dtype=jnp.int32) - start[e_sorted]
    p = TOP_K * t + n_exp * tm
    row_token = jnp.zeros((p,), jnp.int32).at[dest_sorted].set(tok[order])
    row_gate = jnp.zeros((p,), F32).at[dest_sorted].set(g_flat[order])
    dest = jnp.zeros((TOP_K * t,), jnp.int32).at[order].set(dest_sorted)
    nt = p // tm
    tile_start = jnp.arange(nt, dtype=jnp.int32) * tm
    total = pad_end[-1]
    last_tile = total // tm - 1
    tile_block = jnp.minimum(jnp.arange(nt, dtype=jnp.int32), last_tile).astype(jnp.int32)
    tile_expert = jnp.minimum(jnp.searchsorted(pad_end, tile_block * tm, side="right"), n_exp - 1).astype(jnp.int32)
    real_end = pad_start + counts
    tile_rows = jnp.where(tile_start < total, jnp.clip(real_end[tile_expert] - tile_start, 0, tm), 0).astype(jnp.int32)
    return row_token, row_gate.reshape(p, 1), dest[:t], dest[t:], tile_expert, tile_block, tile_rows


def _rope_table(pos):
    half = ROPE // 2
    inv_freq = jnp.exp(-math.log(ROPE_THETA) * jnp.arange(half, dtype=F32) / half)
    ang = pos.astype(F32)[:, None] * inv_freq[None, :]
    cos, sin = jnp.cos(ang), jnp.sin(ang)
    return jnp.concatenate([cos, cos, cos, cos, -sin, sin, -sin, sin], axis=-1)


def kernel(x_prompt, x_sample, p_prompt, p_sample, cache_ckv, cache_krope, state_conv, page_table, g_mix, w_in, g_qa, w_uq, g_qn, g_kva, g_kn, w_uk, w_uv, conv_w, g_ao, g_co, w_o, g_ffn, w_ffn_gate, w_ffn_up, w_ffn_down, w_router, moe_gate, moe_up, moe_down, w_pe, g_pe, g_pg, w_pg):
    b, s, d = x_prompt.shape
    db, ds, _ = x_sample.shape
    depth = w_in.shape[0]
    qlr = g_qa.shape[-1]
    nh = w_uq.shape[2]
    c_dim = conv_w.shape[-1]
    groups = g_co.shape[1]
    n_exp = w_router.shape[-1]
    assert w_uq.shape[3] == NOPE + ROPE and g_kva.shape[-1] == KVR and g_kn.shape[-1] == ROPE
    assert w_uv.shape[-1] == VH and cache_ckv.shape[-1] == KVR and cache_krope.shape[-1] == ROPE
    assert c_dim // groups == LANE and nh * VH == c_dim
    tp, tsmp = b * s, db * ds
    past_len = page_table.shape[1] * cache_ckv.shape[2]
    q_scale = float((NOPE + ROPE) ** -0.5) * LOG2E
    c0 = qlr + KVR + ROPE
    z1w = qlr + KVR + LANE
    moe_tm = 512

    h = jnp.concatenate([x_prompt.reshape(tp, d), x_sample.reshape(tsmp, d)], axis=0)
    p_all = jnp.concatenate([p_prompt.reshape(depth, tp, -1), p_sample.reshape(depth, tsmp, -1)], axis=1)
    tab_p = _rope_table(jnp.arange(s))
    tab_s = jnp.tile(_rope_table(past_len + jnp.arange(ds)), (LANE // ds, 1))
    w_in_t = jnp.swapaxes(w_in, 1, 2)
    cache_krope_t = jnp.swapaxes(cache_krope, 2, 3)

    outs = [[] for _ in range(6)]
    for i in range(depth):
        wq = w_uq[i]
        wuq = jnp.concatenate([wq, wq[..., NOPE:]], axis=-1).reshape(qlr, nh * QH).astype(BF16)
        wuk_t = jnp.transpose(w_uk[i], (1, 2, 0)).astype(BF16)
        wuv_t = jnp.transpose(w_uv[i], (1, 0, 2)).astype(BF16)
        gq = jnp.concatenate([g_qn[i], g_qn[i, NOPE:]]).reshape(1, QH)
        gk = jnp.concatenate([g_kn[i], g_kn[i]]).reshape(1, LANE)
        g_ao_i = g_ao[i].reshape(nh, 1, VH)

        (a,) = rmsnorm(h, g_mix[i])
        z1 = mm_nt(a, w_in_t, i, 0, z1w, F32, tn_pref=384)
        z2 = mm_nt(a, w_in_t, i, c0, 3 * c_dim, BF16)
        q_p, ckv_p, kr_p, kk_p = qkv_post(z1, 0, tp, tab_p, s, wuq, wuk_t, g_qa[i], gq, g_kva[i], gk,
                                          nh, False, q_scale)
        q_s, ckv_s, kr_s = qkv_post(z1, tp, tsmp, tab_s, tab_s.shape[0], wuq, wuk_t, g_qa[i], gq, g_kva[i], gk,
                                    nh, True, q_scale)
        o_p = attn_prompt(q_p, kk_p, wuv_t, g_ao_i, b, s, nh)
        ol_s = attn_sample(q_s, ckv_s, kr_s, cache_ckv, cache_krope_t, page_table, i, nh)
        o_s = ouv(ol_s, wuv_t, g_ao_i, _tile(db, 64, 1), ds)
        yc_p, cv_p = conv_prompt(z2, b, s, conv_w[i], g_co[i], groups)
        yc_s, cv_s = conv_sample(z2, tp, db, ds, state_conv[i], conv_w[i], g_co[i], groups)
        o_all = jnp.concatenate([o_p, o_s], axis=0)
        yc_all = jnp.concatenate([yc_p, yc_s], axis=0)
        h = mm2_res(o_all, yc_all, w_o, (i,), h)

        j = i // 2
        if i % 2 == 0:
            (f,) = rmsnorm(h, g_ffn[i])
            act = mm_swiglu(f, w_ffn_gate, w_ffn_up, (j,))
            h = mm_down(act, w_ffn_down, (j,), h)
        else:
            f, logits = rmsnorm(h, g_ffn[i], router_w=w_router[j], y_dtype=F32)
            row_token, row_gate, pos0, pos1, t_exp, t_blk, t_val = moe_route(logits, n_exp, moe_tm)
            xs = gather_rows(f, row_token, t_val, moe_tm)
            act = moe_up_call(xs, moe_gate, moe_up, j, t_exp, t_blk, t_val, moe_tm)
            ys = moe_down_call(act, moe_down, j, row_gate, t_exp, t_blk, t_val, moe_tm)
            h = moe_combine(ys, pos0, pos1, h)

        (hn,) = rmsnorm(h, g_pg[i])
        e = ple_embed(p_all[i], w_pe, i, g_pe[i])
        h = mm_ple(hn, w_pg, (i,), h, e)

        for lst, v in zip(outs, (ckv_p, kr_p, cv_p, ckv_s, kr_s, cv_s)):
            lst.append(v)

    ckv_prompt = jnp.stack(outs[0]).reshape(depth, b, s, KVR)
    krope_prompt = jnp.stack(outs[1]).reshape(depth, b, s, ROPE)
    conv_prompt_out = jnp.stack(outs[2])
    ckv_sample = jnp.stack(outs[3]).reshape(depth, db, ds, KVR)
    krope_sample = jnp.stack(outs[4]).reshape(depth, db, ds, ROPE)
    conv_sample_out = jnp.stack(outs[5])
    return (h[:tp].reshape(b, s, d), h[tp:].reshape(db, ds, d), ckv_prompt, krope_prompt, conv_prompt_out,
            ckv_sample, krope_sample, conv_sample_out)
```
